```python
import math
import jax, jax.numpy as jnp
from jax import lax
import numpy as np

D_MODEL = 1024
BATCH = 2
SEQ = 8192
DEPTH = 2

BLOCK = 128
EPS = 1e-6
ML_HEADS = 4
ML_HEAD_DIM = 128
ML_WIDTH = ML_HEADS * ML_HEAD_DIM
ML_CONV = 4
SG_GROUPS = 4
SG_GROUP_DIM = 128
SG_WIDTH = SG_GROUPS * SG_GROUP_DIM
FOX_HEADS = 4
FOX_HEAD_DIM = 128
FOX_WIDTH = FOX_HEADS * FOX_HEAD_DIM
DIFF_HEADS = 4
DIFF_QK_DIM = 64
DIFF_V_DIM = 128
DIFF_WIDTH = DIFF_HEADS * DIFF_V_DIM
ROPE_THETA = 500000.0
ROPE_DIMS = DIFF_QK_DIM // 4
D_FF = 2816
N_EXPERTS = 8
TOP_K = 2
D_FF_EXPERT = 3584

EVEN_SPLITS = (ML_WIDTH, ML_WIDTH, ML_WIDTH, ML_WIDTH, 2 * ML_HEADS, SG_WIDTH, SG_WIDTH)
ODD_SPLITS = (FOX_WIDTH, FOX_WIDTH, FOX_WIDTH, FOX_HEADS,
              DIFF_HEADS * 2 * DIFF_QK_DIM, DIFF_HEADS * 2 * DIFF_QK_DIM, DIFF_WIDTH)
EVEN_PROJ = sum(EVEN_SPLITS)
ODD_PROJ = sum(ODD_SPLITS)

kernel_name = "hybrid_mlstm_gmlp_fox_diff_moe"


def split_cols(z, sizes):
    idx = np.cumsum(sizes)[:-1].tolist()
    return jnp.split(z, idx, axis=-1)


def rms_norm(x, g):
    xf = x.astype(jnp.float32)
    y = xf * lax.rsqrt(jnp.mean(xf * xf, axis=-1, keepdims=True) + EPS)
    return (y * g.astype(jnp.float32)).astype(x.dtype)


def causal_dwconv(x, w, b):
    C = x.shape[-1]
    y = lax.conv_general_dilated(x, w[:, None, :].astype(x.dtype), window_strides=(1,),
                                 padding=((w.shape[0] - 1, 0),),
                                 dimension_numbers=("NWC", "WIO", "NWC"),
                                 feature_group_count=C)
    return y + b


def swiglu(t, w_gate_up, w_down):
    g, u = jnp.split(t @ w_gate_up, 2, axis=-1)
    return (jax.nn.silu(g) * u) @ w_down


def partial_rope(x):
    S = x.shape[-2]
    half = ROPE_DIMS // 2
    inv = ROPE_THETA ** (-jnp.arange(half, dtype=jnp.float32) / half)
    ang = jnp.arange(S, dtype=jnp.float32)[:, None] * inv[None, :]
    cos, sin = jnp.cos(ang), jnp.sin(ang)
    xf = x[..., :ROPE_DIMS].astype(jnp.float32)
    x1, x2 = xf[..., :half], xf[..., half:]
    rot = jnp.concatenate([x1 * cos - x2 * sin, x2 * cos + x1 * sin], axis=-1)
    return jnp.concatenate([rot.astype(x.dtype), x[..., ROPE_DIMS:]], axis=-1)


def mlstm_chunkwise(q, k, v, i_pre, f_pre):
    B, H, S, D = q.shape
    L = BLOCK
    NC = S // L
    out_dtype = v.dtype
    f32 = jnp.float32
    q = q.astype(f32) * (D ** -0.5)
    k = k.astype(f32)
    v = v.astype(f32)
    log_f = jax.nn.log_sigmoid(f_pre.astype(f32))
    log_i = i_pre.astype(f32)

    def to_chunks(a):
        return jnp.moveaxis(a.reshape(B, H, NC, L, *a.shape[3:]), 2, 0)

    causal = jnp.tril(jnp.ones((L, L), dtype=bool))

    def step(carry, inp):
        C, n, m = carry
        qb, kb, vb, lfb, lib = inp
        b = jnp.cumsum(lfb, axis=-1)
        dmat = jnp.where(causal, b[..., :, None] - b[..., None, :] + lib[..., None, :], -jnp.inf)
        inter = b + m[..., None]
        m_row = jnp.maximum(inter, jnp.max(dmat, axis=-1))
        w_intra = jnp.exp(dmat - m_row[..., None])
        w_inter = jnp.exp(inter - m_row)
        s = jnp.einsum('bhtd,bhsd->bhts', qb, kb) * w_intra
        num = jnp.einsum('bhts,bhsd->bhtd', s, vb) + w_inter[..., None] * jnp.einsum('bhed,bhtd->bhte', C, qb)
        den = jnp.sum(s, axis=-1) + w_inter * jnp.einsum('bhd,bhtd->bht', n, qb)
        h = num / jnp.maximum(jnp.abs(den), jnp.exp(-m_row))[..., None]
        b_last = b[..., -1]
        g = b_last[..., None] - b + lib
        m_new = jnp.maximum(b_last + m, jnp.max(g, axis=-1))
        w_s = jnp.exp(g - m_new[..., None])
        decay = jnp.exp(b_last + m - m_new)
        C_new = decay[..., None, None] * C + jnp.einsum('bhs,bhse,bhsd->bhed', w_s, vb, kb)
        n_new = decay[..., None] * n + jnp.einsum('bhs,bhsd->bhd', w_s, kb)
        return (C_new, n_new, m_new), h

    init = (jnp.zeros((B, H, D, D), f32), jnp.zeros((B, H, D), f32), jnp.zeros((B, H), f32))
    _, hs = lax.scan(step, init, (to_chunks(q), to_chunks(k), to_chunks(v),
                                  to_chunks(log_f), to_chunks(log_i)))
    return jnp.moveaxis(hs, 0, 2).reshape(B, H, S, D).astype(out_dtype)


def spatial_gating(u, v, w_s, b_s):
    B, S, G, Dg = v.shape
    L = BLOCK
    NC = S // L
    tril = jnp.tril(jnp.ones((L, L), dtype=bool))
    w = jnp.where(tril, w_s, jnp.zeros_like(w_s))
    mixed = jnp.einsum('gts,bnsgc->bntgc', w, v.reshape(B, NC, L, G, Dg)) + b_s.T[None, None, :, :, None]
    return u * mixed.reshape(B, S, G, Dg)


def blocked_causal_attention(q, k, v, scale, log_fcum=None):
    B, H, G, S, Dk = q.shape
    Dv = v.shape[-1]
    NB = S // BLOCK
    q_blocks = jnp.moveaxis(q.reshape(B, H, G, NB, BLOCK, Dk), 3, 0)
    k_pos = jnp.arange(S)
    xs = (q_blocks, jnp.arange(NB))
    if log_fcum is not None:
        xs = xs + (jnp.moveaxis(log_fcum.reshape(B, H, NB, BLOCK), 2, 0),)

    def one_block(args):
        q_blk, idx = args[0], args[1]
        q_pos = idx * BLOCK + jnp.arange(BLOCK)
        logits = jnp.einsum('bhgqd,bhgkd->bhgqk', q_blk, k).astype(jnp.float32) * scale
        if log_fcum is not None:
            fq = args[2]
            logits = logits + (fq[..., :, None] - log_fcum[..., None, :])[:, :, None]
        mask = k_pos[None, :] <= q_pos[:, None]
        logits = jnp.where(mask, logits, -jnp.inf)
        p = jax.nn.softmax(logits, axis=-1)
        return jnp.einsum('bhgqk,bhkv->bhgqv', p.astype(v.dtype), v)

    out = lax.map(one_block, xs)
    return jnp.moveaxis(out, 0, 3).reshape(B, H, G, S, Dv)


def even_mixer(h, w_in, ml_conv_w, ml_conv_b, ml_gate_b, ml_norm_g, sg_norm_g, sg_w, sg_b, w_out):
    B, S, _ = h.shape
    q, k, v, o, gates, u, vs = split_cols(h @ w_in, EVEN_SPLITS)
    qk = jax.nn.silu(causal_dwconv(jnp.concatenate([q, k], axis=-1), ml_conv_w, ml_conv_b))
    q, k = jnp.split(qk, 2, axis=-1)

    def heads(t):
        return t.reshape(B, S, ML_HEADS, ML_HEAD_DIM).transpose(0, 2, 1, 3)

    i_pre = (gates[..., :ML_HEADS] + ml_gate_b[0]).transpose(0, 2, 1)
    f_pre = (gates[..., ML_HEADS:] + ml_gate_b[1]).transpose(0, 2, 1)
    hm = mlstm_chunkwise(heads(q), heads(k), heads(v), i_pre, f_pre).transpose(0, 2, 1, 3)
    hm = rms_norm(hm, ml_norm_g.reshape(ML_HEADS, ML_HEAD_DIM))
    y_ml = jax.nn.sigmoid(o) * hm.reshape(B, S, ML_WIDTH)
    u = jax.nn.gelu(u)
    vs = rms_norm(jax.nn.gelu(vs), sg_norm_g)
    y_sg = spatial_gating(u.reshape(B, S, SG_GROUPS, SG_GROUP_DIM),
                          vs.reshape(B, S, SG_GROUPS, SG_GROUP_DIM), sg_w, sg_b).reshape(B, S, SG_WIDTH)
    return jnp.concatenate([y_ml, y_sg], axis=-1) @ w_out


def odd_mixer(h, w_in, fox_f_b, diff_lambda, diff_norm_g, w_out, lambda_init):
    B, S, _ = h.shape
    fq, fk, fv, ff, dq, dk, dv = split_cols(h @ w_in, ODD_SPLITS)
    log_f = jax.nn.log_sigmoid(ff.astype(jnp.float32) + fox_f_b.astype(jnp.float32))
    F = jnp.cumsum(log_f, axis=1).transpose(0, 2, 1)

    def fheads(t):
        return t.reshape(B, S, FOX_HEADS, FOX_HEAD_DIM).transpose(0, 2, 1, 3)

    y_fox = blocked_causal_attention(fheads(fq)[:, :, None], fheads(fk)[:, :, None], fheads(fv),
                                     FOX_HEAD_DIM ** -0.5, F)[:, :, 0]
    y_fox = y_fox.transpose(0, 2, 1, 3).reshape(B, S, FOX_WIDTH)
    def dheads(t):
        return t.reshape(B, S, DIFF_HEADS, 2, DIFF_QK_DIM).transpose(0, 2, 3, 1, 4)

    qd = partial_rope(dheads(dq))
    kd = partial_rope(dheads(dk))
    vd = dv.reshape(B, S, DIFF_HEADS, DIFF_V_DIM).transpose(0, 2, 1, 3)
    a = blocked_causal_attention(qd, kd, vd, DIFF_QK_DIM ** -0.5)
    lam_f = diff_lambda.astype(jnp.float32)
    lam = (jnp.exp(jnp.sum(lam_f[0] * lam_f[1])) - jnp.exp(jnp.sum(lam_f[2] * lam_f[3])) + lambda_init)
    y_d = a[:, :, 0] - lam.astype(a.dtype) * a[:, :, 1]
    y_d = rms_norm(y_d, diff_norm_g) * (1.0 - lambda_init)
    y_d = y_d.transpose(0, 2, 1, 3).reshape(B, S, DIFF_WIDTH)
    return jnp.concatenate([y_fox, y_d], axis=-1) @ w_out


def moe_swiglu(h, w_router, w_gate_up, w_down):
    B, S, D = h.shape
    t = h.reshape(B * S, D)
    logits = (t @ w_router).astype(jnp.float32)
    top_v, top_i = lax.top_k(logits, TOP_K)
    top_w = jax.nn.softmax(top_v, axis=-1)
    gates = jnp.sum(jax.nn.one_hot(top_i, N_EXPERTS, dtype=jnp.float32) * top_w[..., None], axis=1)
    out = jnp.zeros_like(t)
    for e in range(N_EXPERTS):
        out = out + gates[:, e:e + 1].astype(t.dtype) * swiglu(t, w_gate_up[e], w_down[e])
    return out.reshape(B, S, D)


def setup_inputs(seed: int = 0) -> dict:
    key = jax.random.key(seed)
    ks = iter(jax.random.split(key, 40))
    NE = (DEPTH + 1) // 2
    NO = DEPTH // 2
    D = D_MODEL

    def nrm(shape, scale):
        return jax.random.normal(next(ks), shape, jnp.float32) * scale

    def gain(shape):
        return 1.0 + nrm(shape, 0.02)

    x = nrm((BATCH, SEQ, D), 1.0)
    ml_gate_b = jnp.concatenate([
        nrm((NE, 1, ML_HEADS), 0.1),
        jnp.linspace(3.0, 6.0, ML_HEADS, dtype=jnp.float32)[None, None, :] + nrm((NE, 1, ML_HEADS), 0.1)], axis=1)
    return {
        "x": x,
        "even_norm_mix": gain((NE, D)),
        "even_w_in": nrm((NE, D, EVEN_PROJ), D ** -0.5),
        "even_ml_conv_w": nrm((NE, ML_CONV, 2 * ML_WIDTH), ML_CONV ** -0.5),
        "even_ml_conv_b": nrm((NE, 2 * ML_WIDTH), 0.02),
        "even_ml_gate_b": ml_gate_b,
        "even_ml_norm_g": gain((NE, ML_WIDTH)),
        "even_sg_norm_g": gain((NE, SG_WIDTH)),
        "even_sg_w": nrm((NE, SG_GROUPS, BLOCK, BLOCK), BLOCK ** -0.5),
        "even_sg_b": 1.0 + nrm((NE, SG_GROUPS, BLOCK), 0.02),
        "even_w_out": nrm((NE, ML_WIDTH + SG_WIDTH, D), (ML_WIDTH + SG_WIDTH) ** -0.5),
        "even_norm_ffn": gain((NE, D)),
        "ffn_w_gate_up": nrm((NE, D, 2 * D_FF), D ** -0.5),
        "ffn_w_down": nrm((NE, D_FF, D), D_FF ** -0.5),
        "odd_norm_mix": gain((NO, D)),
        "odd_w_in": nrm((NO, D, ODD_PROJ), D ** -0.5),
        "odd_fox_f_b": jnp.linspace(2.0, 5.0, FOX_HEADS, dtype=jnp.float32)[None, :] + nrm((NO, FOX_HEADS), 0.1),
        "odd_diff_lambda": nrm((NO, 4, DIFF_QK_DIM), 0.1),
        "odd_diff_norm_g": gain((NO, DIFF_V_DIM)),
        "odd_w_out": nrm((NO, FOX_WIDTH + DIFF_WIDTH, D), (FOX_WIDTH + DIFF_WIDTH) ** -0.5),
        "odd_norm_ffn": gain((NO, D)),
        "moe_w_router": nrm((NO, D, N_EXPERTS), D ** -0.5),
        "moe_w_gate_up": nrm((NO, N_EXPERTS, D, 2 * D_FF_EXPERT), D ** -0.5),
        "moe_w_down": nrm((NO, N_EXPERTS, D_FF_EXPERT, D), D_FF_EXPERT ** -0.5),
        "final_norm": gain((D,)),
    }


def reference(x, even_norm_mix, even_w_in, even_ml_conv_w, even_ml_conv_b, even_ml_gate_b,
              even_ml_norm_g, even_sg_norm_g, even_sg_w, even_sg_b, even_w_out, even_norm_ffn,
              ffn_w_gate_up, ffn_w_down, odd_norm_mix, odd_w_in, odd_fox_f_b, odd_diff_lambda,
              odd_diff_norm_g, odd_w_out, odd_norm_ffn, moe_w_router, moe_w_gate_up, moe_w_down,
              final_norm):
    for layer in range(DEPTH):
        p = layer // 2
        if layer % 2 == 0:
            x = x + even_mixer(rms_norm(x, even_norm_mix[p]), even_w_in[p], even_ml_conv_w[p],
                               even_ml_conv_b[p], even_ml_gate_b[p], even_ml_norm_g[p],
                               even_sg_norm_g[p], even_sg_w[p], even_sg_b[p], even_w_out[p])
            x = x + swiglu(rms_norm(x, even_norm_ffn[p]), ffn_w_gate_up[p], ffn_w_down[p])
        else:
            lambda_init = 0.8 - 0.6 * math.exp(-0.3 * layer)
            x = x + odd_mixer(rms_norm(x, odd_norm_mix[p]), odd_w_in[p], odd_fox_f_b[p],
                              odd_diff_lambda[p], odd_diff_norm_g[p], odd_w_out[p], lambda_init)
            x = x + moe_swiglu(rms_norm(x, odd_norm_ffn[p]), moe_w_router[p], moe_w_gate_up[p], moe_w_down[p])
    return rms_norm(x, final_norm)
```

```python
import functools
import math

import jax
import jax.numpy as jnp
from jax import lax
from jax.experimental import pallas as pl
from jax.experimental.pallas import tpu as pltpu

F32 = jnp.float32
BF16 = jnp.bfloat16
HIGHEST = lax.Precision.HIGHEST

EPS = 1e-6
CHUNK = 128
HEAD_DIM = 128
N_HEADS = 4
MIX_WIDTH = N_HEADS * HEAD_DIM
CONV_TAPS = 4
DIFF_QK = 64
ROPE_DIMS = DIFF_QK // 4
ROPE_THETA = 500000.0
N_EXPERTS = 8
LANES = 128
NEG_INF = float("-inf")

VMEM_LIMIT_DEFAULT = 48 * 1024 * 1024
VMEM_LIMIT_MOE = 60 * 1024 * 1024

TM_PROJ = 512
TM_MOE = 512
TM_COMBINE = 256
FF_CHUNK_DENSE = 256
FF_CHUNK_MOE = 512
ATT_BLOCK = 512


def _params(semantics, vmem=VMEM_LIMIT_DEFAULT):
    return pltpu.CompilerParams(dimension_semantics=semantics, vmem_limit_bytes=vmem)


def _rms(x, g):
    return x * lax.rsqrt(jnp.mean(x * x, axis=-1, keepdims=True) + EPS) * g


def _sigmoid(x):
    return 1.0 / (1.0 + jnp.exp(-x))


def _log_sigmoid(x):
    return jnp.minimum(x, 0.0) - jnp.log1p(jnp.exp(-jnp.abs(x)))


def _gelu_tanh(x):
    return 0.5 * x * (1.0 + jnp.tanh(math.sqrt(2.0 / math.pi) * (x + 0.044715 * (x * x * x))))


def _norm_proj_kernel(x_ref, g_ref, w_ref, wg_ref, z_ref, gz_ref, *, col_chunk):
    hn = _rms(x_ref[...], g_ref[...]).astype(BF16)
    for c in range(w_ref.shape[1] // col_chunk):
        cols = slice(c * col_chunk, (c + 1) * col_chunk)
        z_ref[:, cols] = jnp.dot(hn, w_ref[:, cols], preferred_element_type=F32).astype(z_ref.dtype)
    gz_ref[...] = jnp.dot(hn, wg_ref[...], preferred_element_type=F32)


def norm_proj(x, g, w_main, w_gate):
    t, d = x.shape
    n = w_main.shape[1]
    tm = min(TM_PROJ, t)
    return pl.pallas_call(
        functools.partial(_norm_proj_kernel, col_chunk=512),
        out_shape=(jax.ShapeDtypeStruct((t, n), BF16), jax.ShapeDtypeStruct((t, LANES), F32)),
        grid=(t // tm,),
        in_specs=[
            pl.BlockSpec((tm, d), lambda i: (i, 0)),
            pl.BlockSpec((1, d), lambda i: (0, 0)),
            pl.BlockSpec((d, n), lambda i: (0, 0)),
            pl.BlockSpec((d, LANES), lambda i: (0, 0)),
        ],
        out_specs=(pl.BlockSpec((tm, n), lambda i: (i, 0)), pl.BlockSpec((tm, LANES), lambda i: (i, 0))),
        compiler_params=_params(("parallel",)),
        name="norm_proj",
    )(x, g, w_main, w_gate)


def _even_mixer_kernel(zq_ref, zk_ref, zv_ref, zo_ref, zu_ref, zs_ref, gt_ref, gb_ref, cw_ref, cb_ref,
                       mlg_ref, sgg_ref, sgw_ref, sgb_ref, y_ref,
                       c_sc, n_sc, m_sc, tail_sc):
    L = CHUNK

    @pl.when(pl.program_id(1) == 0)
    def _():
        c_sc[...] = jnp.zeros_like(c_sc)
        n_sc[...] = jnp.zeros_like(n_sc)
        m_sc[...] = jnp.zeros_like(m_sc)
        tail_sc[...] = jnp.zeros_like(tail_sc)

    qk_raw = jnp.concatenate([zq_ref[...], zk_ref[...]], axis=1).astype(F32)
    ext = jnp.concatenate([tail_sc[...], qk_raw], axis=0)
    conv = cb_ref[...] + cw_ref[CONV_TAPS - 1:CONV_TAPS, :] * qk_raw
    for j in range(CONV_TAPS - 1):
        lo = 8 - (CONV_TAPS - 1) + j
        conv = conv + cw_ref[j:j + 1, :] * ext[lo:lo + L, :]
    tail_sc[...] = qk_raw[L - 8:, :]
    qk = conv * _sigmoid(conv)
    q_all = qk[:, :MIX_WIDTH] * (HEAD_DIM ** -0.5)
    k_all = qk[:, MIX_WIDTH:]

    gates = gt_ref[...] + gb_ref[...]
    row = lax.broadcasted_iota(jnp.int32, (8, L), 0)
    lgate = jnp.where(row < N_HEADS, gates, _log_sigmoid(gates))
    src = lax.broadcasted_iota(jnp.int32, (L, L), 0)
    dst = lax.broadcasted_iota(jnp.int32, (L, L), 1)
    upper = (src <= dst).astype(F32)
    csum = jnp.dot(lgate, upper, precision=HIGHEST, preferred_element_type=F32)
    rows8 = jnp.where(row < N_HEADS, gates, csum)
    cols = jnp.concatenate([rows8, jnp.zeros((L - 8, L), F32)], axis=0).T
    causal = dst <= src

    for h in range(N_HEADS):
        hs = slice(h * HEAD_DIM, (h + 1) * HEAD_DIM)
        li_row = rows8[h:h + 1, :]
        b_row = rows8[N_HEADS + h:N_HEADS + h + 1, :]
        li_col = cols[:, h:h + 1]
        b_col = cols[:, N_HEADS + h:N_HEADS + h + 1]
        m_prev = m_sc[h][0:1, 0:1]
        c_prev = c_sc[h]
        n_prev = n_sc[h][0:1, :]

        qh = q_all[:, hs]
        kh = k_all[:, hs]
        vh = zv_ref[:, hs]
        qh16 = qh.astype(BF16)
        kh16 = kh.astype(BF16)

        dmat = jnp.where(causal, b_col - b_row + li_row, NEG_INF)
        inter = b_col + m_prev
        m_row = jnp.maximum(inter, jnp.max(dmat, axis=-1, keepdims=True))
        w_intra = jnp.exp(dmat - m_row)
        w_inter = jnp.exp(inter - m_row)
        s = lax.dot_general(qh16, kh16, (((1,), (1,)), ((), ())), preferred_element_type=F32) * w_intra
        cq = lax.dot_general(qh16, c_prev.astype(BF16), (((1,), (1,)), ((), ())), preferred_element_type=F32)
        num = jnp.dot(s.astype(BF16), vh, preferred_element_type=F32) + w_inter * cq
        den = jnp.sum(s, axis=-1, keepdims=True) + w_inter * jnp.sum(qh * n_prev, axis=-1, keepdims=True)
        hm = num / jnp.maximum(jnp.abs(den), jnp.exp(-m_row))

        b_last = b_row[:, L - 1:L]
        g_row = b_last - b_row + li_row
        m_new = jnp.maximum(b_last + m_prev, jnp.max(g_row, axis=-1, keepdims=True))
        w_col = jnp.exp(b_last - b_col + li_col - m_new)
        decay = jnp.exp(b_last + m_prev - m_new)
        vw = (vh.astype(F32) * w_col).astype(BF16)
        c_sc[h] = decay * c_prev + lax.dot_general(vw, kh16, (((0,), (0,)), ((), ())), preferred_element_type=F32)
        n_new = decay * n_prev + jnp.sum(kh * w_col, axis=0, keepdims=True)
        n_sc[h] = jnp.broadcast_to(n_new, (8, HEAD_DIM))
        m_sc[h] = jnp.broadcast_to(m_new, (8, LANES))

        hm = _rms(hm, mlg_ref[:, hs])
        y_ref[:, hs] = (_sigmoid(zo_ref[:, hs].astype(F32)) * hm).astype(y_ref.dtype)

    u = _gelu_tanh(zu_ref[...].astype(F32))
    vs = _rms(_gelu_tanh(zs_ref[...].astype(F32)), sgg_ref[...]).astype(BF16)
    for g in range(N_HEADS):
        gs = slice(g * HEAD_DIM, (g + 1) * HEAD_DIM)
        wg = jnp.where(causal, sgw_ref[g], 0.0).astype(BF16)
        mixed = jnp.dot(wg, vs[:, gs], preferred_element_type=F32) + sgb_ref[:, g:g + 1]
        y_ref[:, MIX_WIDTH + g * HEAD_DIM:MIX_WIDTH + (g + 1) * HEAD_DIM] = (u[:, gs] * mixed).astype(y_ref.dtype)


def even_mixer_core(z, gates_t, gate_bias_col, conv_w, conv_b, ml_norm_g, sg_norm_g, sg_w, sg_b_t, batch):
    t = z.shape[0]
    nc = t // batch // CHUNK
    w = MIX_WIDTH

    def zspec(col):
        return pl.BlockSpec((CHUNK, w), lambda b, c, col=col: (b * nc + c, col))

    full = lambda shape: pl.BlockSpec(shape, lambda b, c: (0,) * len(shape))
    return pl.pallas_call(
        _even_mixer_kernel,
        out_shape=jax.ShapeDtypeStruct((t, 2 * w), BF16),
        grid=(batch, nc),
        in_specs=[zspec(0), zspec(1), zspec(2), zspec(3), zspec(4), zspec(5),
                  pl.BlockSpec((8, CHUNK), lambda b, c: (0, b * nc + c)),
                  full((8, 1)), full((CONV_TAPS, 2 * w)), full((1, 2 * w)),
                  full((1, w)), full((1, w)), full((N_HEADS, CHUNK, CHUNK)), full((CHUNK, N_HEADS))],
        out_specs=pl.BlockSpec((CHUNK, 2 * w), lambda b, c: (b * nc + c, 0)),
        scratch_shapes=[pltpu.VMEM((N_HEADS, HEAD_DIM, HEAD_DIM), F32),
                        pltpu.VMEM((N_HEADS, 8, HEAD_DIM), F32),
                        pltpu.VMEM((N_HEADS, 8, LANES), F32),
                        pltpu.VMEM((8, 2 * w), F32)],
        compiler_params=_params(("parallel", "arbitrary")),
        name="even_mixer",
    )(z, z, z, z, z, z, gates_t, gate_bias_col, conv_w, conv_b, ml_norm_g, sg_norm_g, sg_w, sg_b_t)


def _proj_residual_kernel(*refs, n_in):
    res_ref = refs[2 * n_in]
    out_ref = refs[2 * n_in + 1]
    acc = res_ref[...]
    for a_ref, w_ref in zip(refs[:n_in], refs[n_in:2 * n_in]):
        acc = acc + jnp.dot(a_ref[...], w_ref[...], preferred_element_type=F32)
    out_ref[...] = acc


def proj_residual(acts, weights, res):
    t, d = res.shape
    tm = min(TM_PROJ, t)
    n_in = len(acts)
    in_specs = [pl.BlockSpec((tm, a.shape[1]), lambda i: (i, 0)) for a in acts]
    in_specs += [pl.BlockSpec(w.shape, lambda i: (0, 0)) for w in weights]
    in_specs += [pl.BlockSpec((tm, d), lambda i: (i, 0))]
    return pl.pallas_call(
        functools.partial(_proj_residual_kernel, n_in=n_in),
        out_shape=jax.ShapeDtypeStruct((t, d), F32),
        grid=(t // tm,),
        in_specs=in_specs,
        out_specs=pl.BlockSpec((tm, d), lambda i: (i, 0)),
        compiler_params=_params(("parallel",)),
        name="proj_residual",
    )(*acts, *weights, res)


def _swiglu_accumulate(hn, wgu_ref, wd_ref, d_ff, chunk):
    acc = None
    for c in range(d_ff // chunk):
        g = jnp.dot(hn, wgu_ref[:, c * chunk:(c + 1) * chunk], preferred_element_type=F32)
        u = jnp.dot(hn, wgu_ref[:, d_ff + c * chunk:d_ff + (c + 1) * chunk], preferred_element_type=F32)
        a = (g * _sigmoid(g) * u).astype(BF16)
        part = jnp.dot(a, wd_ref[c * chunk:(c + 1) * chunk, :], preferred_element_type=F32)
        acc = part if acc is None else acc + part
    return acc


def _dense_ffn_kernel(x_ref, g_ref, wgu_ref, wd_ref, out_ref, *, d_ff, chunk):
    x = x_ref[...]
    hn = _rms(x, g_ref[...]).astype(BF16)
    out_ref[...] = x + _swiglu_accumulate(hn, wgu_ref, wd_ref, d_ff, chunk)


def dense_ffn(x, g, w_gate_up, w_down):
    t, d = x.shape
    d_ff = w_down.shape[0]
    tm = min(TM_PROJ, t)
    chunk = FF_CHUNK_DENSE if d_ff % FF_CHUNK_DENSE == 0 else d_ff
    return pl.pallas_call(
        functools.partial(_dense_ffn_kernel, d_ff=d_ff, chunk=chunk),
        out_shape=jax.ShapeDtypeStruct((t, d), F32),
        grid=(t // tm,),
        in_specs=[pl.BlockSpec((tm, d), lambda i: (i, 0)),
                  pl.BlockSpec((1, d), lambda i: (0, 0)),
                  pl.BlockSpec(w_gate_up.shape, lambda i: (0, 0)),
                  pl.BlockSpec(w_down.shape, lambda i: (0, 0))],
        out_specs=pl.BlockSpec((tm, d), lambda i: (i, 0)),
        compiler_params=_params(("parallel",), VMEM_LIMIT_MOE),
        name="dense_ffn",
    )(x, g, w_gate_up, w_down)


def _odd_prep_kernel(dq_ref, dk_ref, gz_ref, fb_ref, ca_ref, sm_ref, sp_ref,
                     q1_ref, q2_ref, kr_ref, f_ref, carry_sc):
    tm = dq_ref.shape[0]

    @pl.when(pl.program_id(1) == 0)
    def _():
        carry_sc[...] = jnp.zeros_like(carry_sc)

    cos_a = ca_ref[...]
    sin_m = sm_ref[...]
    sin_p = sp_ref[...]
    lane = lax.broadcasted_iota(jnp.int32, (tm, LANES), 1)
    first_map = lane < DIFF_QK
    for h in range(N_HEADS):
        hs = slice(h * LANES, (h + 1) * LANES)
        for src_ref, is_q in ((dq_ref, True), (dk_ref, False)):
            x = src_ref[:, hs].astype(F32)
            rot = (x * cos_a + pltpu.roll(x, LANES - ROPE_DIMS // 2, axis=1) * sin_m
                   + pltpu.roll(x, ROPE_DIMS // 2, axis=1) * sin_p)
            if is_q:
                q1_ref[:, hs] = jnp.where(first_map, rot, 0.0).astype(q1_ref.dtype)
                q2_ref[:, hs] = jnp.where(first_map, 0.0, rot).astype(q2_ref.dtype)
            else:
                kr_ref[:, hs] = rot.astype(kr_ref.dtype)

    lf = _log_sigmoid(gz_ref[...] + fb_ref[...])
    r = lax.broadcasted_iota(jnp.int32, (tm, tm), 0)
    c = lax.broadcasted_iota(jnp.int32, (tm, tm), 1)
    lower = (c <= r).astype(F32)
    fcum = jnp.dot(lower, lf, precision=HIGHEST, preferred_element_type=F32) + carry_sc[0:1, :]
    f_ref[...] = fcum
    carry_sc[...] = jnp.broadcast_to(fcum[tm - 1:tm, :], carry_sc.shape)


def odd_prep(z, gz, f_bias_row, cos_a, sin_m, sin_p, batch):
    t = z.shape[0]
    seq = t // batch
    tm = min(TM_PROJ, seq)
    ns = seq // tm
    w = MIX_WIDTH
    tok = lambda col: pl.BlockSpec((tm, w), lambda b, s, col=col: (b * ns + s, col))
    tab = pl.BlockSpec((tm, LANES), lambda b, s: (s, 0))
    out_tok = pl.BlockSpec((tm, w), lambda b, s: (b * ns + s, 0))
    return pl.pallas_call(
        _odd_prep_kernel,
        out_shape=(jax.ShapeDtypeStruct((t, w), BF16), jax.ShapeDtypeStruct((t, w), BF16),
                   jax.ShapeDtypeStruct((t, w), BF16), jax.ShapeDtypeStruct((t, LANES), F32)),
        grid=(batch, ns),
        in_specs=[tok(3), tok(4),
                  pl.BlockSpec((tm, LANES), lambda b, s: (b * ns + s, 0)),
                  pl.BlockSpec((1, LANES), lambda b, s: (0, 0)),
                  tab, tab, tab],
        out_specs=(out_tok, out_tok, out_tok, pl.BlockSpec((tm, LANES), lambda b, s: (b * ns + s, 0))),
        scratch_shapes=[pltpu.VMEM((8, LANES), F32)],
        compiler_params=_params(("parallel", "arbitrary")),
        name="odd_prep",
    )(z, z, gz, f_bias_row, cos_a, sin_m, sin_p)


def _attention_kernel(*refs, n_maps, has_bias, tq, lambda_init):
    idx = 0
    q_refs = refs[idx:idx + n_maps]; idx += n_maps
    k_ref, v_ref = refs[idx:idx + 2]; idx += 2
    if has_bias:
        fcol_ref, frow_ref = refs[idx:idx + 2]; idx += 2
    else:
        lam_ref, ng_ref = refs[idx:idx + 2]; idx += 2
    out_ref = refs[idx]; idx += 1
    m_sc, l_sc, acc_sc = refs[idx:idx + 3]

    h = pl.program_id(1)
    i = pl.program_id(2)
    rows = n_maps * tq
    q = q_refs[0][...] if n_maps == 1 else jnp.concatenate([r[...] for r in q_refs], axis=0)
    if has_bias:
        lane = lax.broadcasted_iota(jnp.int32, (tq, LANES), 1)
        fq = jnp.sum(jnp.where(lane == h, fcol_ref[...], 0.0), axis=-1, keepdims=True)

    m_sc[...] = jnp.full_like(m_sc, -1e30)
    l_sc[...] = jnp.zeros_like(l_sc)
    acc_sc[...] = jnp.zeros_like(acc_sc)

    def block(j, masked):
        start = pl.multiple_of(j * tq, tq)
        k = k_ref[pl.ds(start, tq), :]
        v = v_ref[pl.ds(start, tq), :]
        s = lax.dot_general(q, k, (((1,), (1,)), ((), ())), preferred_element_type=F32)
        if has_bias:
            s = s + (fq - frow_ref[0, j])
        if masked:
            r = lax.broadcasted_iota(jnp.int32, (rows, tq), 0)
            c = lax.broadcasted_iota(jnp.int32, (rows, tq), 1)
            if n_maps > 1:
                r = jnp.where(r >= tq, r - tq, r)
            s = jnp.where(c <= r, s, NEG_INF)
        m_prev = m_sc[...]
        m_new = jnp.maximum(m_prev, jnp.max(s, axis=-1, keepdims=True))
        alpha = jnp.exp(m_prev - m_new)
        p = jnp.exp(s - m_new)
        l_sc[...] = alpha * l_sc[...] + jnp.sum(p, axis=-1, keepdims=True)
        acc_sc[...] = alpha * acc_sc[...] + jnp.dot(p.astype(BF16), v, preferred_element_type=F32)
        m_sc[...] = m_new

    def body(j, carry):
        block(j, False)
        return carry

    lax.fori_loop(0, i, body, 0)
    block(i, True)

    o = acc_sc[...] / l_sc[...]
    if has_bias:
        out_ref[...] = o.astype(out_ref.dtype)
    else:
        lam_p = lam_ref[...]
        lam = (jnp.exp(jnp.sum(lam_p[0:1, :] * lam_p[1:2, :], axis=-1, keepdims=True))
               - jnp.exp(jnp.sum(lam_p[2:3, :] * lam_p[3:4, :], axis=-1, keepdims=True)) + lambda_init)
        y = o[:tq, :] - lam * o[tq:, :]
        out_ref[...] = (_rms(y, ng_ref[...]) * (1.0 - lambda_init)).astype(out_ref.dtype)


def causal_attention(q_list, q_col0, k, k_col0, v, v_col0, batch, *, bias=None, diff=None):
    t = q_list[0].shape[0]
    seq = t // batch
    tq = min(ATT_BLOCK, seq)
    nq = seq // tq
    n_maps = len(q_list)
    qspec = pl.BlockSpec((tq, LANES), lambda b, h, i: (b * nq + i, q_col0 + h))
    kspec = pl.BlockSpec((seq, LANES), lambda b, h, i: (b, k_col0 + h))
    vspec = pl.BlockSpec((seq, LANES), lambda b, h, i: (b, v_col0 + h))
    in_specs = [qspec] * n_maps + [kspec, vspec]
    args = list(q_list) + [k, v]
    if bias is not None:
        fcol, frow = bias
        in_specs += [pl.BlockSpec((tq, LANES), lambda b, h, i: (b * nq + i, 0)),
                     pl.BlockSpec((1, nq, 1, tq), lambda b, h, i: (b * N_HEADS + h, 0, 0, 0))]
        args += [fcol, frow]
        lambda_init = 0.0
    else:
        lam_p, ng, lambda_init = diff
        in_specs += [pl.BlockSpec(lam_p.shape, lambda b, h, i: (0, 0)),
                     pl.BlockSpec(ng.shape, lambda b, h, i: (0, 0))]
        args += [lam_p, ng]
    rows = n_maps * tq
    return pl.pallas_call(
        functools.partial(_attention_kernel, n_maps=n_maps, has_bias=bias is not None, tq=tq,
                          lambda_init=lambda_init),
        out_shape=jax.ShapeDtypeStruct((t, MIX_WIDTH), BF16),
        grid=(batch, N_HEADS, nq),
        in_specs=in_specs,
        out_specs=pl.BlockSpec((tq, LANES), lambda b, h, i: (b * nq + i, h)),
        scratch_shapes=[pltpu.VMEM((rows, 1), F32), pltpu.VMEM((rows, 1), F32), pltpu.VMEM((rows, LANES), F32)],
        compiler_params=_params(("parallel", "parallel", "arbitrary")),
        name="fox_attention" if bias is not None else "diff_attention",
    )(*args)


def _router_kernel(x_ref, g_ref, wr_ref, hp_ref, meta_ref, cnt_ref, carry_sc):
    tm = x_ref.shape[0]

    @pl.when(pl.program_id(0) == 0)
    def _():
        carry_sc[...] = jnp.zeros_like(carry_sc)

    hn = _rms(x_ref[...], g_ref[...])
    hp_ref[...] = hn
    lane = lax.broadcasted_iota(jnp.int32, (tm, LANES), 1)
    logits = jnp.dot(hn, wr_ref[...], precision=HIGHEST, preferred_element_type=F32)
    logits = jnp.where(lane < N_EXPERTS, logits, NEG_INF)
    m1 = jnp.max(logits, axis=-1, keepdims=True)
    i1 = jnp.min(jnp.where(logits == m1, lane, LANES), axis=-1, keepdims=True)
    rest = jnp.where(lane == i1, NEG_INF, logits)
    m2 = jnp.max(rest, axis=-1, keepdims=True)
    i2 = jnp.min(jnp.where(rest == m2, lane, LANES), axis=-1, keepdims=True)
    e2 = jnp.exp(m2 - m1)
    w1 = 1.0 / (1.0 + e2)
    w2 = e2 * w1

    sel1 = lane == i1
    sel2 = lane == i2
    onehot = jnp.where(sel1, 1.0, 0.0) + jnp.where(sel2, 1.0, 0.0)
    r = lax.broadcasted_iota(jnp.int32, (tm, tm), 0)
    c = lax.broadcasted_iota(jnp.int32, (tm, tm), 1)
    lower = (c <= r).astype(BF16)
    incl = jnp.dot(lower, onehot.astype(BF16), preferred_element_type=F32)
    before = incl - onehot + carry_sc[0:1, :]
    r1 = jnp.sum(jnp.where(sel1, before, 0.0), axis=-1, keepdims=True)
    r2 = jnp.sum(jnp.where(sel2, before, 0.0), axis=-1, keepdims=True)
    total = incl[tm - 1:tm, :] + carry_sc[0:1, :]
    carry_sc[...] = jnp.broadcast_to(total, carry_sc.shape)
    cnt_ref[...] = jnp.broadcast_to(total, cnt_ref.shape)

    meta = jnp.where(lane == 0, i1.astype(F32), 0.0)
    meta = jnp.where(lane == 1, i2.astype(F32), meta)
    meta = jnp.where(lane == 2, r1, meta)
    meta = jnp.where(lane == 3, r2, meta)
    meta = jnp.where(lane == 4, w1, meta)
    meta = jnp.where(lane == 5, w2, meta)
    meta_ref[...] = meta


def moe_router(x, g, w_router_pad):
    t, d = x.shape
    tm = min(TM_PROJ, t)
    return pl.pallas_call(
        _router_kernel,
        out_shape=(jax.ShapeDtypeStruct((t, d), F32), jax.ShapeDtypeStruct((t, LANES), F32),
                   jax.ShapeDtypeStruct((8, LANES), F32)),
        grid=(t // tm,),
        in_specs=[pl.BlockSpec((tm, d), lambda i: (i, 0)),
                  pl.BlockSpec((1, d), lambda i: (0, 0)),
                  pl.BlockSpec((d, LANES), lambda i: (0, 0))],
        out_specs=(pl.BlockSpec((tm, d), lambda i: (i, 0)), pl.BlockSpec((tm, LANES), lambda i: (i, 0)),
                   pl.BlockSpec((8, LANES), lambda i: (0, 0))),
        scratch_shapes=[pltpu.VMEM((8, LANES), F32)],
        compiler_params=_params(("arbitrary",)),
        name="moe_router",
    )(x, g, w_router_pad)


SCATTER_CHUNK = 256


def _scatter_rows_kernel(pos_ref, hp_ref, xs_in_ref, xs_ref, sem, *, n_tok):
    del xs_in_ref
    base = pl.program_id(0) * SCATTER_CHUNK

    def copy(tok, k):
        dst = pos_ref[k * n_tok + tok]
        return pltpu.make_async_copy(hp_ref.at[pl.ds(tok, 1)], xs_ref.at[pl.ds(dst, 1)], sem)

    def start(n, carry):
        copy(base + n, 0).start()
        copy(base + n, 1).start()
        return carry

    def wait(n, carry):
        copy(base + n, 0).wait()
        copy(base + n, 1).wait()
        return carry

    lax.fori_loop(0, SCATTER_CHUNK, start, 0)
    lax.fori_loop(0, SCATTER_CHUNK, wait, 0)


def moe_scatter_rows(pos, hp, xs_zero):
    t = hp.shape[0]
    return pl.pallas_call(
        functools.partial(_scatter_rows_kernel, n_tok=t),
        out_shape=jax.ShapeDtypeStruct(xs_zero.shape, xs_zero.dtype),
        grid_spec=pltpu.PrefetchScalarGridSpec(
            num_scalar_prefetch=1,
            grid=(t // SCATTER_CHUNK,),
            in_specs=[pl.BlockSpec(memory_space=pl.ANY), pl.BlockSpec(memory_space=pl.ANY)],
            out_specs=pl.BlockSpec(memory_space=pl.ANY),
            scratch_shapes=[pltpu.SemaphoreType.DMA(())]),
        input_output_aliases={2: 0},
        compiler_params=pltpu.CompilerParams(dimension_semantics=("arbitrary",), has_side_effects=True),
        name="moe_scatter_rows",
    )(pos, hp, xs_zero)


def _expert_ffn_kernel(te_ref, nt_ref, x_ref, wgu_ref, wd_ref, y_ref, *, d_ff, chunk):
    del te_ref
    i = pl.program_id(0)

    @pl.when(i < nt_ref[0])
    def _():
        hn = x_ref[...].astype(BF16)
        y_ref[...] = _swiglu_accumulate(hn, wgu_ref.at[0], wd_ref.at[0], d_ff, chunk)

    @pl.when(i >= nt_ref[0])
    def _():
        y_ref[...] = jnp.zeros_like(y_ref)


def moe_expert_ffn(tile_expert, n_tiles, xs, w_gate_up, w_down):
    r, d = xs.shape
    d_ff = w_down.shape[1]
    tm = min(TM_MOE, r)
    chunk = FF_CHUNK_MOE if d_ff % FF_CHUNK_MOE == 0 else d_ff
    return pl.pallas_call(
        functools.partial(_expert_ffn_kernel, d_ff=d_ff, chunk=chunk),
        out_shape=jax.ShapeDtypeStruct((r, d), F32),
        grid_spec=pltpu.PrefetchScalarGridSpec(
            num_scalar_prefetch=2,
            grid=(r // tm,),
            in_specs=[pl.BlockSpec((tm, d), lambda i, te, nt: (i, 0)),
                      pl.BlockSpec((1, d, 2 * d_ff), lambda i, te, nt: (te[i], 0, 0)),
                      pl.BlockSpec((1, d_ff, d), lambda i, te, nt: (te[i], 0, 0))],
            out_specs=pl.BlockSpec((tm, d), lambda i, te, nt: (i, 0))),
        compiler_params=_params(("arbitrary",), VMEM_LIMIT_MOE),
        name="moe_expert_ffn",
    )(tile_expert, n_tiles, xs, w_gate_up, w_down)


def _combine_kernel(pos_ref, x_ref, meta_ref, g_ref, ys_ref, out_ref, buf, sem, *, n_tok):
    tm = x_ref.shape[0]
    base = pl.program_id(0) * tm

    def copy(n, k):
        src = pos_ref[k * n_tok + base + n]
        return pltpu.make_async_copy(ys_ref.at[pl.ds(src, 1)], buf.at[k, pl.ds(n, 1)], sem)

    def start(n, carry):
        copy(n, 0).start()
        copy(n, 1).start()
        return carry

    def wait(n, carry):
        copy(n, 0).wait()
        copy(n, 1).wait()
        return carry

    lax.fori_loop(0, tm, start, 0)
    lax.fori_loop(0, tm, wait, 0)
    meta = meta_ref[...]
    y = x_ref[...] + meta[:, 4:5] * buf[0] + meta[:, 5:6] * buf[1]
    out_ref[...] = _rms(y, g_ref[...])


def moe_combine(pos, x, meta, g_final, ys):
    t, d = x.shape
    tm = min(TM_COMBINE, t)
    return pl.pallas_call(
        functools.partial(_combine_kernel, n_tok=t),
        out_shape=jax.ShapeDtypeStruct((t, d), F32),
        grid_spec=pltpu.PrefetchScalarGridSpec(
            num_scalar_prefetch=1,
            grid=(t // tm,),
            in_specs=[pl.BlockSpec((tm, d), lambda i, pos: (i, 0)),
                      pl.BlockSpec((tm, LANES), lambda i, pos: (i, 0)),
                      pl.BlockSpec((1, d), lambda i, pos: (0, 0)),
                      pl.BlockSpec(memory_space=pl.ANY)],
            out_specs=pl.BlockSpec((tm, d), lambda i, pos: (i, 0)),
            scratch_shapes=[pltpu.VMEM((2, tm, d), F32), pltpu.SemaphoreType.DMA(())]),
        compiler_params=_params(("arbitrary",)),
        name="moe_combine",
    )(pos, x, meta, g_final, ys)


def moe_layer(x, g_ffn, w_router, w_gate_up, w_down, g_final):
    t, d = x.shape
    n_exp = w_router.shape[1]
    tm = min(TM_MOE, 2 * t)
    n_rows = 2 * t + n_exp * tm
    wr_pad = jnp.zeros((d, LANES), F32).at[:, :n_exp].set(w_router)
    hp, meta, counts = moe_router(x, g_ffn, wr_pad)

    cnt = counts[0, :n_exp].astype(jnp.int32)
    padded = ((cnt + tm - 1) // tm) * tm
    ends = jnp.cumsum(padded)
    starts = ends - padded
    e_idx = meta[:, 0:2].astype(jnp.int32)
    rank = meta[:, 2:4].astype(jnp.int32)
    start_of = jnp.sum(jnp.where(e_idx[..., None] == jnp.arange(n_exp), starts, 0), axis=-1)
    pos = (start_of + rank).T.reshape(-1)
    tile_start = jnp.arange(n_rows // tm, dtype=jnp.int32) * tm
    tile_expert = jnp.minimum(jnp.sum(tile_start[:, None] >= ends[None, :], axis=-1), n_exp - 1).astype(jnp.int32)
    n_tiles = (ends[-1:] // tm).astype(jnp.int32)

    xs = moe_scatter_rows(pos, hp, jnp.zeros((n_rows, d), F32))
    ys = moe_expert_ffn(tile_expert, n_tiles, xs, w_gate_up, w_down)
    return moe_combine(pos, x, meta, g_final, ys)


def _rope_tables(seq):
    half = ROPE_DIMS // 2
    inv = ROPE_THETA ** (-jnp.arange(half, dtype=F32) / half)
    ang = jnp.arange(seq, dtype=F32)[:, None] * inv[None, :]
    cos, sin = jnp.cos(ang), jnp.sin(ang)
    ones = jnp.ones((seq, DIFF_QK - ROPE_DIMS), F32)
    zeros = jnp.zeros((seq, DIFF_QK - ROPE_DIMS), F32)
    zh = jnp.zeros((seq, half), F32)
    cos_a = jnp.concatenate([cos, cos, ones], axis=1)
    sin_m = jnp.concatenate([-sin, zh, zeros], axis=1)
    sin_p = jnp.concatenate([zh, sin, zeros], axis=1)
    tile2 = lambda a: jnp.concatenate([a, a], axis=1)
    return tile2(cos_a), tile2(sin_m), tile2(sin_p)


def even_layer(x, batch, norm_mix, w_in, conv_w, conv_b, gate_b, ml_norm_g, sg_norm_g, sg_w, sg_b, w_out,
               norm_ffn, w_gate_up, w_down):
    w = MIX_WIDTH
    n_gate = 2 * N_HEADS
    w_main = jnp.concatenate([w_in[:, :4 * w], w_in[:, 4 * w + n_gate:]], axis=1).astype(BF16)
    w_gate = jnp.zeros((w_in.shape[0], LANES), F32).at[:, :n_gate].set(w_in[:, 4 * w:4 * w + n_gate]).astype(BF16)
    z, gz = norm_proj(x, norm_mix[None, :], w_main, w_gate)
    gates_t = gz[:, :n_gate].T
    y = even_mixer_core(z, gates_t, gate_b.reshape(n_gate, 1), conv_w, conv_b[None, :], ml_norm_g[None, :],
                        sg_norm_g[None, :], sg_w, sg_b.T, batch)
    x = proj_residual([y], [w_out.astype(BF16)], x)
    return dense_ffn(x, norm_ffn[None, :], w_gate_up.astype(BF16), w_down.astype(BF16))


def odd_mixer_layer(x, batch, norm_mix, w_in, fox_f_b, diff_lambda, diff_norm_g, w_out, lambda_init):
    w = MIX_WIDTH
    t = x.shape[0]
    seq = t // batch
    w_main = jnp.concatenate([w_in[:, :w] * (HEAD_DIM ** -0.5), w_in[:, w:3 * w],
                              w_in[:, 3 * w + N_HEADS:4 * w + N_HEADS] * (DIFF_QK ** -0.5),
                              w_in[:, 4 * w + N_HEADS:]], axis=1).astype(BF16)
    w_gate = jnp.zeros((w_in.shape[0], LANES), F32).at[:, :N_HEADS].set(w_in[:, 3 * w:3 * w + N_HEADS]).astype(BF16)
    z, gz = norm_proj(x, norm_mix[None, :], w_main, w_gate)
    f_bias_row = jnp.zeros((1, LANES), F32).at[0, :N_HEADS].set(fox_f_b)
    cos_a, sin_m, sin_p = _rope_tables(seq)
    q1, q2, kr, fcol = odd_prep(z, gz, f_bias_row, cos_a, sin_m, sin_p, batch)
    tq = min(ATT_BLOCK, seq)
    frow = fcol[:, :N_HEADS].reshape(batch, seq, N_HEADS).transpose(0, 2, 1).reshape(
        batch * N_HEADS, seq // tq, 1, tq)
    y_fox = causal_attention([z], 0, z, N_HEADS, z, 2 * N_HEADS, batch, bias=(fcol, frow))
    y_diff = causal_attention([q1, q2], 0, kr, 0, z, 5 * N_HEADS, batch,
                              diff=(diff_lambda, diff_norm_g[None, :], lambda_init))
    w_out16 = w_out.astype(BF16)
    return proj_residual([y_fox, y_diff], [w_out16[:w], w_out16[w:]], x)


def kernel(x, even_norm_mix, even_w_in, even_ml_conv_w, even_ml_conv_b, even_ml_gate_b, even_ml_norm_g,
           even_sg_norm_g, even_sg_w, even_sg_b, even_w_out, even_norm_ffn, ffn_w_gate_up, ffn_w_down,
           odd_norm_mix, odd_w_in, odd_fox_f_b, odd_diff_lambda, odd_diff_norm_g, odd_w_out, odd_norm_ffn,
           moe_w_router, moe_w_gate_up, moe_w_down, final_norm):
    batch, seq, d = x.shape
    h = x.reshape(batch * seq, d)
    h = even_layer(h, batch, even_norm_mix[0], even_w_in[0], even_ml_conv_w[0], even_ml_conv_b[0],
                   even_ml_gate_b[0], even_ml_norm_g[0], even_sg_norm_g[0], even_sg_w[0], even_sg_b[0],
                   even_w_out[0], even_norm_ffn[0], ffn_w_gate_up[0], ffn_w_down[0])
    lambda_init = 0.8 - 0.6 * math.exp(-0.3 * 1)
    h = odd_mixer_layer(h, batch, odd_norm_mix[0], odd_w_in[0], odd_fox_f_b[0], odd_diff_lambda[0],
                        odd_diff_norm_g[0], odd_w_out[0], lambda_init)
    out = moe_layer(h, odd_norm_ffn[0][None, :], moe_w_router[0], moe_w_gate_up[0].astype(BF16),
                    moe_w_down[0].astype(BF16), final_norm[None, :])
    return out.reshape(batch, seq, d)
```

```python
import functools
import math

import jax
import jax.numpy as jnp
from jax import lax
from jax.experimental import pallas as pl
from jax.experimental.pallas import tpu as pltpu

F32 = jnp.float32
BF16 = jnp.bfloat16
HIGHEST = lax.Precision.HIGHEST

EPS = 1e-6
CHUNK = 128
HEAD_DIM = 128
N_HEADS = 4
MIX_WIDTH = N_HEADS * HEAD_DIM
CONV_TAPS = 4
DIFF_QK = 64
ROPE_DIMS = DIFF_QK // 4
ROPE_THETA = 500000.0
N_EXPERTS = 8
LANES = 128
NEG_INF = float("-inf")

VMEM_LIMIT_DEFAULT = 48 * 1024 * 1024
VMEM_LIMIT_MOE = 60 * 1024 * 1024

TM_PROJ = 512
TM_MOE = 512
TM_COMBINE = 256
FF_CHUNK_DENSE = 256
FF_CHUNK_MOE = 512
ATT_BLOCK = 512


def _params(semantics, vmem=VMEM_LIMIT_DEFAULT):
    return pltpu.CompilerParams(dimension_semantics=semantics, vmem_limit_bytes=vmem)


def _rms(x, g):
    return x * lax.rsqrt(jnp.mean(x * x, axis=-1, keepdims=True) + EPS) * g


def _sigmoid(x):
    return 1.0 / (1.0 + jnp.exp(-x))


def _log_sigmoid(x):
    return jnp.minimum(x, 0.0) - jnp.log1p(jnp.exp(-jnp.abs(x)))


def _gelu_tanh(x):
    return 0.5 * x * (1.0 + jnp.tanh(math.sqrt(2.0 / math.pi) * (x + 0.044715 * (x * x * x))))


def _norm_proj_kernel(x_ref, g_ref, w_ref, wg_ref, z_ref, gz_ref, *, col_chunk):
    hn = _rms(x_ref[...], g_ref[...]).astype(BF16)
    for c in range(w_ref.shape[1] // col_chunk):
        cols = slice(c * col_chunk, (c + 1) * col_chunk)
        z_ref[:, cols] = jnp.dot(hn, w_ref[:, cols], preferred_element_type=F32).astype(z_ref.dtype)
    gz_ref[...] = jnp.dot(hn, wg_ref[...], preferred_element_type=F32)


def norm_proj(x, g, w_main, w_gate):
    t, d = x.shape
    n = w_main.shape[1]
    tm = min(TM_PROJ, t)
    return pl.pallas_call(
        functools.partial(_norm_proj_kernel, col_chunk=512),
        out_shape=(jax.ShapeDtypeStruct((t, n), BF16), jax.ShapeDtypeStruct((t, LANES), F32)),
        grid=(t // tm,),
        in_specs=[
            pl.BlockSpec((tm, d), lambda i: (i, 0)),
            pl.BlockSpec((1, d), lambda i: (0, 0)),
            pl.BlockSpec((d, n), lambda i: (0, 0)),
            pl.BlockSpec((d, LANES), lambda i: (0, 0)),
        ],
        out_specs=(pl.BlockSpec((tm, n), lambda i: (i, 0)), pl.BlockSpec((tm, LANES), lambda i: (i, 0))),
        compiler_params=_params(("parallel",)),
        name="norm_proj",
    )(x, g, w_main, w_gate)


def _even_mixer_kernel(zq_ref, zk_ref, zv_ref, zo_ref, zu_ref, zs_ref, gt_ref, gb_ref, cw_ref, cb_ref,
                       mlg_ref, sgg_ref, sgw_ref, sgb_ref, y_ref,
                       c_sc, n_sc, m_sc, tail_sc):
    L = CHUNK

    @pl.when(pl.program_id(1) == 0)
    def _():
        c_sc[...] = jnp.zeros_like(c_sc)
        n_sc[...] = jnp.zeros_like(n_sc)
        m_sc[...] = jnp.zeros_like(m_sc)
        tail_sc[...] = jnp.zeros_like(tail_sc)

    qk_raw = jnp.concatenate([zq_ref[...], zk_ref[...]], axis=1).astype(F32)
    ext = jnp.concatenate([tail_sc[...], qk_raw], axis=0)
    conv = cb_ref[...] + cw_ref[CONV_TAPS - 1:CONV_TAPS, :] * qk_raw
    for j in range(CONV_TAPS - 1):
        lo = 8 - (CONV_TAPS - 1) + j
        conv = conv + cw_ref[j:j + 1, :] * ext[lo:lo + L, :]
    tail_sc[...] = qk_raw[L - 8:, :]
    qk = conv * _sigmoid(conv)
    q_all = qk[:, :MIX_WIDTH] * (HEAD_DIM ** -0.5)
    k_all = qk[:, MIX_WIDTH:]

    gates = gt_ref[...] + gb_ref[...]
    row = lax.broadcasted_iota(jnp.int32, (8, L), 0)
    lgate = jnp.where(row < N_HEADS, gates, _log_sigmoid(gates))
    src = lax.broadcasted_iota(jnp.int32, (L, L), 0)
    dst = lax.broadcasted_iota(jnp.int32, (L, L), 1)
    upper = (src <= dst).astype(F32)
    csum = jnp.dot(lgate, upper, precision=HIGHEST, preferred_element_type=F32)
    rows8 = jnp.where(row < N_HEADS, gates, csum)
    cols = jnp.concatenate([rows8, jnp.zeros((L - 8, L), F32)], axis=0).T
    causal = dst <= src

    for h in range(N_HEADS):
        hs = slice(h * HEAD_DIM, (h + 1) * HEAD_DIM)
        li_row = rows8[h:h + 1, :]
        b_row = rows8[N_HEADS + h:N_HEADS + h + 1, :]
        li_col = cols[:, h:h + 1]
        b_col = cols[:, N_HEADS + h:N_HEADS + h + 1]
        m_prev = m_sc[h][0:1, 0:1]
        c_prev = c_sc[h]
        n_prev = n_sc[h][0:1, :]

        qh = q_all[:, hs]
        kh = k_all[:, hs]
        vh = zv_ref[:, hs]
        qh16 = qh.astype(BF16)
        kh16 = kh.astype(BF16)

        dmat = jnp.where(causal, b_col - b_row + li_row, NEG_INF)
        inter = b_col + m_prev
        m_row = jnp.maximum(inter, jnp.max(dmat, axis=-1, keepdims=True))
        w_intra = jnp.exp(dmat - m_row)
        w_inter = jnp.exp(inter - m_row)
        s = lax.dot_general(qh16, kh16, (((1,), (1,)), ((), ())), preferred_element_type=F32) * w_intra
        cq = lax.dot_general(qh16, c_prev.astype(BF16), (((1,), (1,)), ((), ())), preferred_element_type=F32)
        num = jnp.dot(s.astype(BF16), vh, preferred_element_type=F32) + w_inter * cq
        den = jnp.sum(s, axis=-1, keepdims=True) + w_inter * jnp.sum(qh * n_prev, axis=-1, keepdims=True)
        hm = num / jnp.maximum(jnp.abs(den), jnp.exp(-m_row))

        b_last = b_row[:, L - 1:L]
        g_row = b_last - b_row + li_row
        m_new = jnp.maximum(b_last + m_prev, jnp.max(g_row, axis=-1, keepdims=True))
        w_col = jnp.exp(b_last - b_col + li_col - m_new)
        decay = jnp.exp(b_last + m_prev - m_new)
        vw = (vh.astype(F32) * w_col).astype(BF16)
        c_sc[h] = decay * c_prev + lax.dot_general(vw, kh16, (((0,), (0,)), ((), ())), preferred_element_type=F32)
        n_new = decay * n_prev + jnp.sum(kh * w_col, axis=0, keepdims=True)
        n_sc[h] = jnp.broadcast_to(n_new, (8, HEAD_DIM))
        m_sc[h] = jnp.broadcast_to(m_new, (8, LANES))

        hm = _rms(hm, mlg_ref[:, hs])
        y_ref[:, hs] = (_sigmoid(zo_ref[:, hs].astype(F32)) * hm).astype(y_ref.dtype)

    u = _gelu_tanh(zu_ref[...].astype(F32))
    vs = _rms(_gelu_tanh(zs_ref[...].astype(F32)), sgg_ref[...]).astype(BF16)
    for g in range(N_HEADS):
        gs = slice(g * HEAD_DIM, (g + 1) * HEAD_DIM)
        wg = jnp.where(causal, sgw_ref[g], 0.0).astype(BF16)
        mixed = jnp.dot(wg, vs[:, gs], preferred_element_type=F32) + sgb_ref[:, g:g + 1]
        y_ref[:, MIX_WIDTH + g * HEAD_DIM:MIX_WIDTH + (g + 1) * HEAD_DIM] = (u[:, gs] * mixed).astype(y_ref.dtype)


def even_mixer_core(z, gates_t, gate_bias_col, conv_w, conv_b, ml_norm_g, sg_norm_g, sg_w, sg_b_t, batch):
    t = z.shape[0]
    nc = t // batch // CHUNK
    w = MIX_WIDTH

    def zspec(col):
        return pl.BlockSpec((CHUNK, w), lambda b, c, col=col: (b * nc + c, col))

    full = lambda shape: pl.BlockSpec(shape, lambda b, c: (0,) * len(shape))
    return pl.pallas_call(
        _even_mixer_kernel,
        out_shape=jax.ShapeDtypeStruct((t, 2 * w), BF16),
        grid=(batch, nc),
        in_specs=[zspec(0), zspec(1), zspec(2), zspec(3), zspec(4), zspec(5),
                  pl.BlockSpec((8, CHUNK), lambda b, c: (0, b * nc + c)),
                  full((8, 1)), full((CONV_TAPS, 2 * w)), full((1, 2 * w)),
                  full((1, w)), full((1, w)), full((N_HEADS, CHUNK, CHUNK)), full((CHUNK, N_HEADS))],
        out_specs=pl.BlockSpec((CHUNK, 2 * w), lambda b, c: (b * nc + c, 0)),
        scratch_shapes=[pltpu.VMEM((N_HEADS, HEAD_DIM, HEAD_DIM), F32),
                        pltpu.VMEM((N_HEADS, 8, HEAD_DIM), F32),
                        pltpu.VMEM((N_HEADS, 8, LANES), F32),
                        pltpu.VMEM((8, 2 * w), F32)],
        compiler_params=_params(("parallel", "arbitrary")),
        name="even_mixer",
    )(z, z, z, z, z, z, gates_t, gate_bias_col, conv_w, conv_b, ml_norm_g, sg_norm_g, sg_w, sg_b_t)


def _proj_residual_kernel(*refs, n_in):
    res_ref = refs[2 * n_in]
    out_ref = refs[2 * n_in + 1]
    acc = res_ref[...]
    for a_ref, w_ref in zip(refs[:n_in], refs[n_in:2 * n_in]):
        acc = acc + jnp.dot(a_ref[...], w_ref[...], preferred_element_type=F32)
    out_ref[...] = acc


def proj_residual(acts, weights, res):
    t, d = res.shape
    tm = min(TM_PROJ, t)
    n_in = len(acts)
    in_specs = [pl.BlockSpec((tm, a.shape[1]), lambda i: (i, 0)) for a in acts]
    in_specs += [pl.BlockSpec(w.shape, lambda i: (0, 0)) for w in weights]
    in_specs += [pl.BlockSpec((tm, d), lambda i: (i, 0))]
    return pl.pallas_call(
        functools.partial(_proj_residual_kernel, n_in=n_in),
        out_shape=jax.ShapeDtypeStruct((t, d), F32),
        grid=(t // tm,),
        in_specs=in_specs,
        out_specs=pl.BlockSpec((tm, d), lambda i: (i, 0)),
        compiler_params=_params(("parallel",)),
        name="proj_residual",
    )(*acts, *weights, res)


def _swiglu_accumulate(hn, wgu_ref, wd_ref, d_ff, chunk):
    acc = None
    for c in range(d_ff // chunk):
        g = jnp.dot(hn, wgu_ref[:, c * chunk:(c + 1) * chunk], preferred_element_type=F32)
        u = jnp.dot(hn, wgu_ref[:, d_ff + c * chunk:d_ff + (c + 1) * chunk], preferred_element_type=F32)
        a = (g * _sigmoid(g) * u).astype(BF16)
        part = jnp.dot(a, wd_ref[c * chunk:(c + 1) * chunk, :], preferred_element_type=F32)
        acc = part if acc is None else acc + part
    return acc


def _dense_ffn_kernel(x_ref, g_ref, wgu_ref, wd_ref, out_ref, *, d_ff, chunk):
    x = x_ref[...]
    hn = _rms(x, g_ref[...]).astype(BF16)
    out_ref[...] = x + _swiglu_accumulate(hn, wgu_ref, wd_ref, d_ff, chunk)


def dense_ffn(x, g, w_gate_up, w_down):
    t, d = x.shape
    d_ff = w_down.shape[0]
    tm = min(TM_PROJ, t)
    chunk = FF_CHUNK_DENSE if d_ff % FF_CHUNK_DENSE == 0 else d_ff
    return pl.pallas_call(
        functools.partial(_dense_ffn_kernel, d_ff=d_ff, chunk=chunk),
        out_shape=jax.ShapeDtypeStruct((t, d), F32),
        grid=(t // tm,),
        in_specs=[pl.BlockSpec((tm, d), lambda i: (i, 0)),
                  pl.BlockSpec((1, d), lambda i: (0, 0)),
                  pl.BlockSpec(w_gate_up.shape, lambda i: (0, 0)),
                  pl.BlockSpec(w_down.shape, lambda i: (0, 0))],
        out_specs=pl.BlockSpec((tm, d), lambda i: (i, 0)),
        compiler_params=_params(("parallel",), VMEM_LIMIT_MOE),
        name="dense_ffn",
    )(x, g, w_gate_up, w_down)


def _odd_prep_kernel(dq_ref, dk_ref, gz_ref, fb_ref, ca_ref, sm_ref, sp_ref,
                     q1_ref, q2_ref, kr_ref, f_ref, carry_sc):
    tm = dq_ref.shape[0]

    @pl.when(pl.program_id(1) == 0)
    def _():
        carry_sc[...] = jnp.zeros_like(carry_sc)

    cos_a = ca_ref[...]
    sin_m = sm_ref[...]
    sin_p = sp_ref[...]
    lane = lax.broadcasted_iota(jnp.int32, (tm, LANES), 1)
    first_map = lane < DIFF_QK
    for h in range(N_HEADS):
        hs = slice(h * LANES, (h + 1) * LANES)
        for src_ref, is_q in ((dq_ref, True), (dk_ref, False)):
            x = src_ref[:, hs].astype(F32)
            rot = (x * cos_a + pltpu.roll(x, LANES - ROPE_DIMS // 2, axis=1) * sin_m
                   + pltpu.roll(x, ROPE_DIMS // 2, axis=1) * sin_p)
            if is_q:
                q1_ref[:, hs] = jnp.where(first_map, rot, 0.0).astype(q1_ref.dtype)
                q2_ref[:, hs] = jnp.where(first_map, 0.0, rot).astype(q2_ref.dtype)
            else:
                kr_ref[:, hs] = rot.astype(kr_ref.dtype)

    lf = _log_sigmoid(gz_ref[...] + fb_ref[...])
    r = lax.broadcasted_iota(jnp.int32, (tm, tm), 0)
    c = lax.broadcasted_iota(jnp.int32, (tm, tm), 1)
    lower = (c <= r).astype(F32)
    fcum = jnp.dot(lower, lf, precision=HIGHEST, preferred_element_type=F32) + carry_sc[0:1, :]
    f_ref[...] = fcum
    carry_sc[...] = jnp.broadcast_to(fcum[tm - 1:tm, :], carry_sc.shape)


def odd_prep(z, gz, f_bias_row, cos_a, sin_m, sin_p, batch):
    t = z.shape[0]
    seq = t // batch
    tm = min(TM_PROJ, seq)
    ns = seq // tm
    w = MIX_WIDTH
    tok = lambda col: pl.BlockSpec((tm, w), lambda b, s, col=col: (b * ns + s, col))
    tab = pl.BlockSpec((tm, LANES), lambda b, s: (s, 0))
    out_tok = pl.BlockSpec((tm, w), lambda b, s: (b * ns + s, 0))
    return pl.pallas_call(
        _odd_prep_kernel,
        out_shape=(jax.ShapeDtypeStruct((t, w), BF16), jax.ShapeDtypeStruct((t, w), BF16),
                   jax.ShapeDtypeStruct((t, w), BF16), jax.ShapeDtypeStruct((t, LANES), F32)),
        grid=(batch, ns),
        in_specs=[tok(3), tok(4),
                  pl.BlockSpec((tm, LANES), lambda b, s: (b * ns + s, 0)),
                  pl.BlockSpec((1, LANES), lambda b, s: (0, 0)),
                  tab, tab, tab],
        out_specs=(out_tok, out_tok, out_tok, pl.BlockSpec((tm, LANES), lambda b, s: (b * ns + s, 0))),
        scratch_shapes=[pltpu.VMEM((8, LANES), F32)],
        compiler_params=_params(("parallel", "arbitrary")),
        name="odd_prep",
    )(z, z, gz, f_bias_row, cos_a, sin_m, sin_p)


def _attention_kernel(*refs, n_maps, has_bias, tq, lambda_init):
    idx = 0
    q_refs = refs[idx:idx + n_maps]; idx += n_maps
    k_ref, v_ref = refs[idx:idx + 2]; idx += 2
    if has_bias:
        fcol_ref, frow_ref = refs[idx:idx + 2]; idx += 2
    else:
        lam_ref, ng_ref = refs[idx:idx + 2]; idx += 2
    out_ref = refs[idx]; idx += 1
    m_sc, l_sc, acc_sc = refs[idx:idx + 3]

    h = pl.program_id(1)
    i = pl.program_id(2)
    rows = n_maps * tq
    q = q_refs[0][...] if n_maps == 1 else jnp.concatenate([r[...] for r in q_refs], axis=0)
    if has_bias:
        lane = lax.broadcasted_iota(jnp.int32, (tq, LANES), 1)
        fq = jnp.sum(jnp.where(lane == h, fcol_ref[...], 0.0), axis=-1, keepdims=True)

    m_sc[...] = jnp.full_like(m_sc, -1e30)
    l_sc[...] = jnp.zeros_like(l_sc)
    acc_sc[...] = jnp.zeros_like(acc_sc)

    def block(j, masked):
        start = pl.multiple_of(j * tq, tq)
        k = k_ref[pl.ds(start, tq), :]
        v = v_ref[pl.ds(start, tq), :]
        s = lax.dot_general(q, k, (((1,), (1,)), ((), ())), preferred_element_type=F32)
        if has_bias:
            s = s + (fq - frow_ref[0, j])
        if masked:
            r = lax.broadcasted_iota(jnp.int32, (rows, tq), 0)
            c = lax.broadcasted_iota(jnp.int32, (rows, tq), 1)
            if n_maps > 1:
                r = jnp.where(r >= tq, r - tq, r)
            s = jnp.where(c <= r, s, NEG_INF)
        m_prev = m_sc[...]
        m_new = jnp.maximum(m_prev, jnp.max(s, axis=-1, keepdims=True))
        alpha = jnp.exp(m_prev - m_new)
        p = jnp.exp(s - m_new)
        l_sc[...] = alpha * l_sc[...] + jnp.sum(p, axis=-1, keepdims=True)
        acc_sc[...] = alpha * acc_sc[...] + jnp.dot(p.astype(BF16), v, preferred_element_type=F32)
        m_sc[...] = m_new

    def body(j, carry):
        block(j, False)
        return carry

    lax.fori_loop(0, i, body, 0)
    block(i, True)

    o = acc_sc[...] / l_sc[...]
    if has_bias:
        out_ref[...] = o.astype(out_ref.dtype)
    else:
        lam_p = lam_ref[...]
        lam = (jnp.exp(jnp.sum(lam_p[0:1, :] * lam_p[1:2, :], axis=-1, keepdims=True))
               - jnp.exp(jnp.sum(lam_p[2:3, :] * lam_p[3:4, :], axis=-1, keepdims=True)) + lambda_init)
        y = o[:tq, :] - lam * o[tq:, :]
        out_ref[...] = (_rms(y, ng_ref[...]) * (1.0 - lambda_init)).astype(out_ref.dtype)


def causal_attention(q_list, q_col0, k, k_col0, v, v_col0, batch, *, bias=None, diff=None):
    t = q_list[0].shape[0]
    seq = t // batch
    tq = min(ATT_BLOCK, seq)
    nq = seq // tq
    n_maps = len(q_list)
    qspec = pl.BlockSpec((tq, LANES), lambda b, h, i: (b * nq + i, q_col0 + h))
    kspec = pl.BlockSpec((seq, LANES), lambda b, h, i: (b, k_col0 + h))
    vspec = pl.BlockSpec((seq, LANES), lambda b, h, i: (b, v_col0 + h))
    in_specs = [qspec] * n_maps + [kspec, vspec]
    args = list(q_list) + [k, v]
    if bias is not None:
        fcol, frow = bias
        in_specs += [pl.BlockSpec((tq, LANES), lambda b, h, i: (b * nq + i, 0)),
                     pl.BlockSpec((1, nq, 1, tq), lambda b, h, i: (b * N_HEADS + h, 0, 0, 0))]
        args += [fcol, frow]
        lambda_init = 0.0
    else:
        lam_p, ng, lambda_init = diff
        in_specs += [pl.BlockSpec(lam_p.shape, lambda b, h, i: (0, 0)),
                     pl.BlockSpec(ng.shape, lambda b, h, i: (0, 0))]
        args += [lam_p, ng]
    rows = n_maps * tq
    return pl.pallas_call(
        functools.partial(_attention_kernel, n_maps=n_maps, has_bias=bias is not None, tq=tq,
                          lambda_init=lambda_init),
        out_shape=jax.ShapeDtypeStruct((t, MIX_WIDTH), BF16),
        grid=(batch, N_HEADS, nq),
        in_specs=in_specs,
        out_specs=pl.BlockSpec((tq, LANES), lambda b, h, i: (b * nq + i, h)),
        scratch_shapes=[pltpu.VMEM((rows, 1), F32), pltpu.VMEM((rows, 1), F32), pltpu.VMEM((rows, LANES), F32)],
        compiler_params=_params(("parallel", "parallel", "arbitrary")),
        name="fox_attention" if bias is not None else "diff_attention",
    )(*args)


def _router_kernel(x_ref, g_ref, wr_ref, hp_ref, meta_ref, cnt_ref, carry_sc):
    tm = x_ref.shape[0]

    @pl.when(pl.program_id(0) == 0)
    def _():
        carry_sc[...] = jnp.zeros_like(carry_sc)

    hn = _rms(x_ref[...], g_ref[...])
    hp_ref[...] = hn
    lane = lax.broadcasted_iota(jnp.int32, (tm, LANES), 1)
    logits = jnp.dot(hn, wr_ref[...], precision=HIGHEST, preferred_element_type=F32)
    logits = jnp.where(lane < N_EXPERTS, logits, NEG_INF)
    m1 = jnp.max(logits, axis=-1, keepdims=True)
    i1 = jnp.min(jnp.where(logits == m1, lane, LANES), axis=-1, keepdims=True)
    rest = jnp.where(lane == i1, NEG_INF, logits)
    m2 = jnp.max(rest, axis=-1, keepdims=True)
    i2 = jnp.min(jnp.where(rest == m2, lane, LANES), axis=-1, keepdims=True)
    e2 = jnp.exp(m2 - m1)
    w1 = 1.0 / (1.0 + e2)
    w2 = e2 * w1

    sel1 = lane == i1
    sel2 = lane == i2
    onehot = jnp.where(sel1, 1.0, 0.0) + jnp.where(sel2, 1.0, 0.0)
    r = lax.broadcasted_iota(jnp.int32, (tm, tm), 0)
    c = lax.broadcasted_iota(jnp.int32, (tm, tm), 1)
    lower = (c <= r).astype(BF16)
    incl = jnp.dot(lower, onehot.astype(BF16), preferred_element_type=F32)
    before = incl - onehot + carry_sc[0:1, :]
    r1 = jnp.sum(jnp.where(sel1, before, 0.0), axis=-1, keepdims=True)
    r2 = jnp.sum(jnp.where(sel2, before, 0.0), axis=-1, keepdims=True)
    total = incl[tm - 1:tm, :] + carry_sc[0:1, :]
    carry_sc[...] = jnp.broadcast_to(total, carry_sc.shape)
    cnt_ref[...] = jnp.broadcast_to(total, cnt_ref.shape)

    meta = jnp.where(lane == 0, i1.astype(F32), 0.0)
    meta = jnp.where(lane == 1, i2.astype(F32), meta)
    meta = jnp.where(lane == 2, r1, meta)
    meta = jnp.where(lane == 3, r2, meta)
    meta = jnp.where(lane == 4, w1, meta)
    meta = jnp.where(lane == 5, w2, meta)
    meta_ref[...] = meta


def moe_router(x, g, w_router_pad):
    t, d = x.shape
    tm = min(TM_PROJ, t)
    return pl.pallas_call(
        _router_kernel,
        out_shape=(jax.ShapeDtypeStruct((t, d), F32), jax.ShapeDtypeStruct((t, LANES), F32),
                   jax.ShapeDtypeStruct((8, LANES), F32)),
        grid=(t // tm,),
        in_specs=[pl.BlockSpec((tm, d), lambda i: (i, 0)),
                  pl.BlockSpec((1, d), lambda i: (0, 0)),
                  pl.BlockSpec((d, LANES), lambda i: (0, 0))],
        out_specs=(pl.BlockSpec((tm, d), lambda i: (i, 0)), pl.BlockSpec((tm, LANES), lambda i: (i, 0)),
                   pl.BlockSpec((8, LANES), lambda i: (0, 0))),
        scratch_shapes=[pltpu.VMEM((8, LANES), F32)],
        compiler_params=_params(("arbitrary",)),
        name="moe_router",
    )(x, g, w_router_pad)


def _expert_ffn_kernel(te_ref, nt_ref, nv_ref, tok_ref, dst_ref, hn_ref, pad_in_ref, wgu_ref, wd_ref,
                       planes_ref, pad_ref, xbuf, ybuf, gsem, ssem, *, d_ff, chunk, tm):
    del te_ref, pad_in_ref
    i = pl.program_id(0)
    nt = nt_ref[0]
    slot = i % 2

    def start_gather(tile, s):
        base = tile * tm

        def body(r, carry):
            pltpu.make_async_copy(hn_ref.at[pl.ds(tok_ref[base + r], 1)], xbuf.at[s, pl.ds(r, 1)], gsem.at[s]).start()
            return carry

        lax.fori_loop(0, tm, body, 0, unroll=8)

    def start_scatter(tile, s):
        base = tile * tm

        def body(out_ref, r, carry):
            pltpu.make_async_copy(ybuf.at[s, pl.ds(r, 1)], out_ref.at[pl.ds(dst_ref[base + r], 1)], ssem.at[s]).start()
            return carry

        lax.fori_loop(0, nv_ref[tile], functools.partial(body, planes_ref), 0)
        lax.fori_loop(nv_ref[tile], tm, functools.partial(body, pad_ref), 0)

    def wait_gather(s):
        pltpu.make_async_copy(hn_ref.at[pl.ds(0, tm)], xbuf.at[s], gsem.at[s]).wait()

    def wait_scatter(s):
        pltpu.make_async_copy(ybuf.at[s], planes_ref.at[pl.ds(0, tm)], ssem.at[s]).wait()

    @pl.when(i == 0)
    def _():
        start_gather(0, 0)

    @pl.when(i + 1 < nt)
    def _():
        start_gather(i + 1, 1 - slot)

    @pl.when(i < nt)
    def _():
        wait_gather(slot)

        @pl.when(i >= 2)
        def _():
            wait_scatter(slot)

        ybuf[slot] = _swiglu_accumulate(xbuf[slot].astype(BF16), wgu_ref.at[0], wd_ref.at[0], d_ff, chunk)
        start_scatter(i, slot)

    @pl.when(i == nt - 1)
    def _():
        wait_scatter(slot)

        @pl.when(i >= 1)
        def _():
            wait_scatter(1 - slot)


def moe_expert_ffn(tile_expert, n_tiles, tile_valid, row_token, row_dst, hn, w_gate_up, w_down):
    t, d = hn.shape
    n_exp, d_ff = w_down.shape[:2]
    tm = min(TM_MOE, 2 * t)
    n_grid = row_token.shape[0] // tm
    chunk = FF_CHUNK_MOE if d_ff % FF_CHUNK_MOE == 0 else d_ff
    w_index = lambda i, te, nt, nv, tok, dst: (te[i], 0, 0)
    planes, _ = pl.pallas_call(
        functools.partial(_expert_ffn_kernel, d_ff=d_ff, chunk=chunk, tm=tm),
        out_shape=(jax.ShapeDtypeStruct((2 * t, d), F32), jax.ShapeDtypeStruct((n_exp * tm, d), F32)),
        grid_spec=pltpu.PrefetchScalarGridSpec(
            num_scalar_prefetch=5,
            grid=(n_grid,),
            in_specs=[pl.BlockSpec(memory_space=pl.ANY), pl.BlockSpec(memory_space=pl.ANY),
                      pl.BlockSpec((1, d, 2 * d_ff), w_index),
                      pl.BlockSpec((1, d_ff, d), w_index)],
            out_specs=(pl.BlockSpec(memory_space=pl.ANY), pl.BlockSpec(memory_space=pl.ANY)),
            scratch_shapes=[pltpu.VMEM((2, tm, d), F32), pltpu.VMEM((2, tm, d), F32),
                            pltpu.SemaphoreType.DMA((2,)), pltpu.SemaphoreType.DMA((2,))]),
        input_output_aliases={6: 1},
        compiler_params=_params(("arbitrary",), VMEM_LIMIT_MOE),
        name="moe_expert_ffn",
    )(tile_expert, n_tiles, tile_valid, row_token, row_dst, hn, jnp.zeros((n_exp * tm, d), F32), w_gate_up, w_down)
    return planes


def _combine_kernel(x_ref, meta_ref, g_ref, p0_ref, p1_ref, out_ref):
    meta = meta_ref[...]
    y = x_ref[...] + meta[:, 4:5] * p0_ref[...] + meta[:, 5:6] * p1_ref[...]
    out_ref[...] = _rms(y, g_ref[...])


def moe_combine(x, meta, g_final, planes):
    t, d = x.shape
    tm = min(TM_PROJ, t)
    nb = t // tm
    return pl.pallas_call(
        _combine_kernel,
        out_shape=jax.ShapeDtypeStruct((t, d), F32),
        grid=(nb,),
        in_specs=[pl.BlockSpec((tm, d), lambda i: (i, 0)),
                  pl.BlockSpec((tm, LANES), lambda i: (i, 0)),
                  pl.BlockSpec((1, d), lambda i: (0, 0)),
                  pl.BlockSpec((tm, d), lambda i: (i, 0)),
                  pl.BlockSpec((tm, d), lambda i: (nb + i, 0))],
        out_specs=pl.BlockSpec((tm, d), lambda i: (i, 0)),
        compiler_params=_params(("parallel",)),
        name="moe_combine",
    )(x, meta, g_final, planes, planes)


def moe_layer(x, g_ffn, w_router, w_gate_up, w_down, g_final):
    t, d = x.shape
    n_exp = w_router.shape[1]
    tm = min(TM_MOE, 2 * t)
    n_rows = 2 * t + n_exp * tm
    wr_pad = jnp.zeros((d, LANES), F32).at[:, :n_exp].set(w_router)
    hn, meta, counts = moe_router(x, g_ffn, wr_pad)

    cnt = counts[0, :n_exp].astype(jnp.int32)
    padded = ((cnt + tm - 1) // tm) * tm
    ends = jnp.cumsum(padded)
    starts = ends - padded
    e_idx = meta[:, 0:2].astype(jnp.int32)
    rank = meta[:, 2:4].astype(jnp.int32)
    start_of = jnp.sum(jnp.where(e_idx[..., None] == jnp.arange(n_exp), starts, 0), axis=-1)
    pos = (start_of + rank).T.reshape(-1)
    tile_start = jnp.arange(n_rows // tm, dtype=jnp.int32) * tm
    tile_expert = jnp.minimum(jnp.sum(tile_start[:, None] >= ends[None, :], axis=-1), n_exp - 1).astype(jnp.int32)
    n_tiles = (ends[-1:] // tm).astype(jnp.int32)
    tile_valid = jnp.clip(cnt[tile_expert] - (tile_start - starts[tile_expert]), 0, tm).astype(jnp.int32)
    pad_dst = (tile_expert[:, None] * tm + jnp.arange(tm, dtype=jnp.int32)[None, :]).reshape(-1)
    served = jnp.full((n_rows,), -1, jnp.int32).at[pos].set(jnp.arange(2 * t, dtype=jnp.int32), unique_indices=True)
    row_dst = jnp.where(served >= 0, served, pad_dst)
    row_token = jnp.where(served >= 0, served % t, 0)

    planes = moe_expert_ffn(tile_expert, n_tiles, tile_valid, row_token, row_dst, hn, w_gate_up, w_down)
    return moe_combine(x, meta, g_final, planes)


def _rope_tables(seq):
    half = ROPE_DIMS // 2
    inv = ROPE_THETA ** (-jnp.arange(half, dtype=F32) / half)
    ang = jnp.arange(seq, dtype=F32)[:, None] * inv[None, :]
    cos, sin = jnp.cos(ang), jnp.sin(ang)
    ones = jnp.ones((seq, DIFF_QK - ROPE_DIMS), F32)
    zeros = jnp.zeros((seq, DIFF_QK - ROPE_DIMS), F32)
    zh = jnp.zeros((seq, half), F32)
    cos_a = jnp.concatenate([cos, cos, ones], axis=1)
    sin_m = jnp.concatenate([-sin, zh, zeros], axis=1)
    sin_p = jnp.concatenate([zh, sin, zeros], axis=1)
    tile2 = lambda a: jnp.concatenate([a, a], axis=1)
    return tile2(cos_a), tile2(sin_m), tile2(sin_p)


def even_layer(x, batch, norm_mix, w_in, conv_w, conv_b, gate_b, ml_norm_g, sg_norm_g, sg_w, sg_b, w_out,
               norm_ffn, w_gate_up, w_down):
    w = MIX_WIDTH
    n_gate = 2 * N_HEADS
    w_main = jnp.concatenate([w_in[:, :4 * w], w_in[:, 4 * w + n_gate:]], axis=1).astype(BF16)
    w_gate = jnp.zeros((w_in.shape[0], LANES), F32).at[:, :n_gate].set(w_in[:, 4 * w:4 * w + n_gate]).astype(BF16)
    z, gz = norm_proj(x, norm_mix[None, :], w_main, w_gate)
    gates_t = gz[:, :n_gate].T
    y = even_mixer_core(z, gates_t, gate_b.reshape(n_gate, 1), conv_w, conv_b[None, :], ml_norm_g[None, :],
                        sg_norm_g[None, :], sg_w, sg_b.T, batch)
    x = proj_residual([y], [w_out.astype(BF16)], x)
    return dense_ffn(x, norm_ffn[None, :], w_gate_up.astype(BF16), w_down.astype(BF16))


def odd_mixer_layer(x, batch, norm_mix, w_in, fox_f_b, diff_lambda, diff_norm_g, w_out, lambda_init):
    w = MIX_WIDTH
    t = x.shape[0]
    seq = t // batch
    w_main = jnp.concatenate([w_in[:, :w] * (HEAD_DIM ** -0.5), w_in[:, w:3 * w],
                              w_in[:, 3 * w + N_HEADS:4 * w + N_HEADS] * (DIFF_QK ** -0.5),
                              w_in[:, 4 * w + N_HEADS:]], axis=1).astype(BF16)
    w_gate = jnp.zeros((w_in.shape[0], LANES), F32).at[:, :N_HEADS].set(w_in[:, 3 * w:3 * w + N_HEADS]).astype(BF16)
    z, gz = norm_proj(x, norm_mix[None, :], w_main, w_gate)
    f_bias_row = jnp.zeros((1, LANES), F32).at[0, :N_HEADS].set(fox_f_b)
    cos_a, sin_m, sin_p = _rope_tables(seq)
    q1, q2, kr, fcol = odd_prep(z, gz, f_bias_row, cos_a, sin_m, sin_p, batch)
    tq = min(ATT_BLOCK, seq)
    frow = fcol[:, :N_HEADS].reshape(batch, seq, N_HEADS).transpose(0, 2, 1).reshape(
        batch * N_HEADS, seq // tq, 1, tq)
    y_fox = causal_attention([z], 0, z, N_HEADS, z, 2 * N_HEADS, batch, bias=(fcol, frow))
    y_diff = causal_attention([q1, q2], 0, kr, 0, z, 5 * N_HEADS, batch,
                              diff=(diff_lambda, diff_norm_g[None, :], lambda_init))
    w_out16 = w_out.astype(BF16)
    return proj_residual([y_fox, y_diff], [w_out16[:w], w_out16[w:]], x)


def kernel(x, even_norm_mix, even_w_in, even_ml_conv_w, even_ml_conv_b, even_ml_gate_b, even_ml_norm_g,
           even_sg_norm_g, even_sg_w, even_sg_b, even_w_out, even_norm_ffn, ffn_w_gate_up, ffn_w_down,
           odd_norm_mix, odd_w_in, odd_fox_f_b, odd_diff_lambda, odd_diff_norm_g, odd_w_out, odd_norm_ffn,
           moe_w_router, moe_w_gate_up, moe_w_down, final_norm):
    batch, seq, d = x.shape
    h = x.reshape(batch * seq, d)
    h = even_layer(h, batch, even_norm_mix[0], even_w_in[0], even_ml_conv_w[0], even_ml_conv_b[0],
                   even_ml_gate_b[0], even_ml_norm_g[0], even_sg_norm_g[0], even_sg_w[0], even_sg_b[0],
                   even_w_out[0], even_norm_ffn[0], ffn_w_gate_up[0], ffn_w_down[0])
    lambda_init = 0.8 - 0.6 * math.exp(-0.3 * 1)
    h = odd_mixer_layer(h, batch, odd_norm_mix[0], odd_w_in[0], odd_fox_f_b[0], odd_diff_lambda[0],
                        odd_diff_norm_g[0], odd_w_out[0], lambda_init)
    out = moe_layer(h, odd_norm_ffn[0][None, :], moe_w_router[0], moe_w_gate_up[0].astype(BF16),
                    moe_w_down[0].astype(BF16), final_norm[None, :])
    return out.reshape(batch, seq, d)
```

```python
import functools
import math

import jax
import jax.numpy as jnp
from jax import lax
from jax.experimental import pallas as pl
from jax.experimental.pallas import tpu as pltpu

F32 = jnp.float32
BF16 = jnp.bfloat16
HIGHEST = lax.Precision.HIGHEST

EPS = 1e-6
CHUNK = 128
HEAD_DIM = 128
N_HEADS = 4
MIX_WIDTH = N_HEADS * HEAD_DIM
CONV_TAPS = 4
DIFF_QK = 64
ROPE_DIMS = DIFF_QK // 4
ROPE_THETA = 500000.0
N_EXPERTS = 8
LANES = 128
NEG_INF = float("-inf")

VMEM_LIMIT_DEFAULT = 48 * 1024 * 1024
VMEM_LIMIT_MOE = 60 * 1024 * 1024

TM_PROJ = 512
TM_MOE = 512
FF_CHUNK_DENSE = 256
FF_CHUNK_MOE = 512
ATT_TQ = 512
ATT_CHUNK = 1024
LOG2E = math.log2(math.e)


def _params(semantics, vmem=VMEM_LIMIT_DEFAULT):
    return pltpu.CompilerParams(dimension_semantics=semantics, vmem_limit_bytes=vmem)


def _rms(x, g):
    return x * lax.rsqrt(jnp.mean(x * x, axis=-1, keepdims=True) + EPS) * g


def _sigmoid(x):
    return 1.0 / (1.0 + jnp.exp(-x))


def _log_sigmoid(x):
    return jnp.minimum(x, 0.0) - jnp.log1p(jnp.exp(-jnp.abs(x)))


def _gelu_tanh(x):
    return 0.5 * x * (1.0 + jnp.tanh(math.sqrt(2.0 / math.pi) * (x + 0.044715 * (x * x * x))))


def _norm_proj_kernel(x_ref, g_ref, w_ref, wg_ref, z_ref, gz_ref, *, col_chunk):
    hn = _rms(x_ref[...], g_ref[...]).astype(BF16)
    for c in range(w_ref.shape[1] // col_chunk):
        cols = slice(c * col_chunk, (c + 1) * col_chunk)
        z_ref[:, cols] = jnp.dot(hn, w_ref[:, cols], preferred_element_type=F32).astype(z_ref.dtype)
    gz_ref[...] = jnp.dot(hn, wg_ref[...], preferred_element_type=F32)


def norm_proj(x, g, w_main, w_gate):
    t, d = x.shape
    n = w_main.shape[1]
    tm = min(TM_PROJ, t)
    return pl.pallas_call(
        functools.partial(_norm_proj_kernel, col_chunk=512),
        out_shape=(jax.ShapeDtypeStruct((t, n), BF16), jax.ShapeDtypeStruct((t, LANES), F32)),
        grid=(t // tm,),
        in_specs=[
            pl.BlockSpec((tm, d), lambda i: (i, 0)),
            pl.BlockSpec((1, d), lambda i: (0, 0)),
            pl.BlockSpec((d, n), lambda i: (0, 0)),
            pl.BlockSpec((d, LANES), lambda i: (0, 0)),
        ],
        out_specs=(pl.BlockSpec((tm, n), lambda i: (i, 0)), pl.BlockSpec((tm, LANES), lambda i: (i, 0))),
        compiler_params=_params(("parallel",)),
        name="norm_proj",
    )(x, g, w_main, w_gate)


def _even_mixer_kernel(zq_ref, zk_ref, zv_ref, zo_ref, zu_ref, zs_ref, gt_ref, gb_ref, cw_ref, cb_ref,
                       mlg_ref, sgg_ref, sgw_ref, sgb_ref, y_ref,
                       c_sc, n_sc, m_sc, tail_sc):
    L = CHUNK

    @pl.when(pl.program_id(1) == 0)
    def _():
        c_sc[...] = jnp.zeros_like(c_sc)
        n_sc[...] = jnp.zeros_like(n_sc)
        m_sc[...] = jnp.zeros_like(m_sc)
        tail_sc[...] = jnp.zeros_like(tail_sc)

    qk_raw = jnp.concatenate([zq_ref[...], zk_ref[...]], axis=1).astype(F32)
    ext = jnp.concatenate([tail_sc[...], qk_raw], axis=0)
    conv = cb_ref[...] + cw_ref[CONV_TAPS - 1:CONV_TAPS, :] * qk_raw
    for j in range(CONV_TAPS - 1):
        lo = 8 - (CONV_TAPS - 1) + j
        conv = conv + cw_ref[j:j + 1, :] * ext[lo:lo + L, :]
    tail_sc[...] = qk_raw[L - 8:, :]
    qk = conv * _sigmoid(conv)
    q_all = qk[:, :MIX_WIDTH] * (HEAD_DIM ** -0.5)
    k_all = qk[:, MIX_WIDTH:]

    gates = gt_ref[...] + gb_ref[...]
    row = lax.broadcasted_iota(jnp.int32, (8, L), 0)
    lgate = jnp.where(row < N_HEADS, gates, _log_sigmoid(gates))
    src = lax.broadcasted_iota(jnp.int32, (L, L), 0)
    dst = lax.broadcasted_iota(jnp.int32, (L, L), 1)
    upper = (src <= dst).astype(F32)
    csum = jnp.dot(lgate, upper, precision=HIGHEST, preferred_element_type=F32)
    rows8 = jnp.where(row < N_HEADS, gates, csum)
    cols = jnp.concatenate([rows8, jnp.zeros((L - 8, L), F32)], axis=0).T
    causal = dst <= src

    for h in range(N_HEADS):
        hs = slice(h * HEAD_DIM, (h + 1) * HEAD_DIM)
        li_row = rows8[h:h + 1, :]
        b_row = rows8[N_HEADS + h:N_HEADS + h + 1, :]
        li_col = cols[:, h:h + 1]
        b_col = cols[:, N_HEADS + h:N_HEADS + h + 1]
        m_prev = m_sc[h][0:1, 0:1]
        c_prev = c_sc[h]
        n_prev = n_sc[h][0:1, :]

        qh = q_all[:, hs]
        kh = k_all[:, hs]
        vh = zv_ref[:, hs]
        qh16 = qh.astype(BF16)
        kh16 = kh.astype(BF16)

        dmat = jnp.where(causal, b_col - b_row + li_row, NEG_INF)
        inter = b_col + m_prev
        m_row = jnp.maximum(inter, jnp.max(dmat, axis=-1, keepdims=True))
        w_intra = jnp.exp(dmat - m_row)
        w_inter = jnp.exp(inter - m_row)
        s = lax.dot_general(qh16, kh16, (((1,), (1,)), ((), ())), preferred_element_type=F32) * w_intra
        cq = lax.dot_general(qh16, c_prev.astype(BF16), (((1,), (1,)), ((), ())), preferred_element_type=F32)
        num = jnp.dot(s.astype(BF16), vh, preferred_element_type=F32) + w_inter * cq
        den = jnp.sum(s, axis=-1, keepdims=True) + w_inter * jnp.sum(qh * n_prev, axis=-1, keepdims=True)
        hm = num / jnp.maximum(jnp.abs(den), jnp.exp(-m_row))

        b_last = b_row[:, L - 1:L]
        g_row = b_last - b_row + li_row
        m_new = jnp.maximum(b_last + m_prev, jnp.max(g_row, axis=-1, keepdims=True))
        w_col = jnp.exp(b_last - b_col + li_col - m_new)
        decay = jnp.exp(b_last + m_prev - m_new)
        vw = (vh.astype(F32) * w_col).astype(BF16)
        c_sc[h] = decay * c_prev + lax.dot_general(vw, kh16, (((0,), (0,)), ((), ())), preferred_element_type=F32)
        n_new = decay * n_prev + jnp.sum(kh * w_col, axis=0, keepdims=True)
        n_sc[h] = jnp.broadcast_to(n_new, (8, HEAD_DIM))
        m_sc[h] = jnp.broadcast_to(m_new, (8, LANES))

        hm = _rms(hm, mlg_ref[:, hs])
        y_ref[:, hs] = (_sigmoid(zo_ref[:, hs].astype(F32)) * hm).astype(y_ref.dtype)

    u = _gelu_tanh(zu_ref[...].astype(F32))
    vs = _rms(_gelu_tanh(zs_ref[...].astype(F32)), sgg_ref[...]).astype(BF16)
    for g in range(N_HEADS):
        gs = slice(g * HEAD_DIM, (g + 1) * HEAD_DIM)
        wg = jnp.where(causal, sgw_ref[g], 0.0).astype(BF16)
        mixed = jnp.dot(wg, vs[:, gs], preferred_element_type=F32) + sgb_ref[:, g:g + 1]
        y_ref[:, MIX_WIDTH + g * HEAD_DIM:MIX_WIDTH + (g + 1) * HEAD_DIM] = (u[:, gs] * mixed).astype(y_ref.dtype)


def even_mixer_core(z, gates_t, gate_bias_col, conv_w, conv_b, ml_norm_g, sg_norm_g, sg_w, sg_b_t, batch):
    t = z.shape[0]
    nc = t // batch // CHUNK
    w = MIX_WIDTH

    def zspec(col):
        return pl.BlockSpec((CHUNK, w), lambda b, c, col=col: (b * nc + c, col))

    full = lambda shape: pl.BlockSpec(shape, lambda b, c: (0,) * len(shape))
    return pl.pallas_call(
        _even_mixer_kernel,
        out_shape=jax.ShapeDtypeStruct((t, 2 * w), BF16),
        grid=(batch, nc),
        in_specs=[zspec(0), zspec(1), zspec(2), zspec(3), zspec(4), zspec(5),
                  pl.BlockSpec((8, CHUNK), lambda b, c: (0, b * nc + c)),
                  full((8, 1)), full((CONV_TAPS, 2 * w)), full((1, 2 * w)),
                  full((1, w)), full((1, w)), full((N_HEADS, CHUNK, CHUNK)), full((CHUNK, N_HEADS))],
        out_specs=pl.BlockSpec((CHUNK, 2 * w), lambda b, c: (b * nc + c, 0)),
        scratch_shapes=[pltpu.VMEM((N_HEADS, HEAD_DIM, HEAD_DIM), F32),
                        pltpu.VMEM((N_HEADS, 8, HEAD_DIM), F32),
                        pltpu.VMEM((N_HEADS, 8, LANES), F32),
                        pltpu.VMEM((8, 2 * w), F32)],
        compiler_params=_params(("parallel", "arbitrary")),
        name="even_mixer",
    )(z, z, z, z, z, z, gates_t, gate_bias_col, conv_w, conv_b, ml_norm_g, sg_norm_g, sg_w, sg_b_t)


def _proj_residual_kernel(*refs, n_in):
    res_ref = refs[2 * n_in]
    out_ref = refs[2 * n_in + 1]
    acc = res_ref[...]
    for a_ref, w_ref in zip(refs[:n_in], refs[n_in:2 * n_in]):
        acc = acc + jnp.dot(a_ref[...], w_ref[...], preferred_element_type=F32)
    out_ref[...] = acc


def proj_residual(acts, weights, res):
    t, d = res.shape
    tm = min(TM_PROJ, t)
    n_in = len(acts)
    in_specs = [pl.BlockSpec((tm, a.shape[1]), lambda i: (i, 0)) for a in acts]
    in_specs += [pl.BlockSpec(w.shape, lambda i: (0, 0)) for w in weights]
    in_specs += [pl.BlockSpec((tm, d), lambda i: (i, 0))]
    return pl.pallas_call(
        functools.partial(_proj_residual_kernel, n_in=n_in),
        out_shape=jax.ShapeDtypeStruct((t, d), F32),
        grid=(t // tm,),
        in_specs=in_specs,
        out_specs=pl.BlockSpec((tm, d), lambda i: (i, 0)),
        compiler_params=_params(("parallel",)),
        name="proj_residual",
    )(*acts, *weights, res)


def _swiglu_accumulate(hn, wgu_ref, wd_ref, d_ff, chunk):
    acc = None
    for c in range(d_ff // chunk):
        g = jnp.dot(hn, wgu_ref[:, c * chunk:(c + 1) * chunk], preferred_element_type=F32)
        u = jnp.dot(hn, wgu_ref[:, d_ff + c * chunk:d_ff + (c + 1) * chunk], preferred_element_type=F32)
        a = (g * _sigmoid(g) * u).astype(BF16)
        part = jnp.dot(a, wd_ref[c * chunk:(c + 1) * chunk, :], preferred_element_type=F32)
        acc = part if acc is None else acc + part
    return acc


def _dense_ffn_kernel(x_ref, g_ref, wgu_ref, wd_ref, out_ref, *, d_ff, chunk):
    x = x_ref[...]
    hn = _rms(x, g_ref[...]).astype(BF16)
    out_ref[...] = x + _swiglu_accumulate(hn, wgu_ref, wd_ref, d_ff, chunk)


def dense_ffn(x, g, w_gate_up, w_down):
    t, d = x.shape
    d_ff = w_down.shape[0]
    tm = min(TM_PROJ, t)
    chunk = FF_CHUNK_DENSE if d_ff % FF_CHUNK_DENSE == 0 else d_ff
    return pl.pallas_call(
        functools.partial(_dense_ffn_kernel, d_ff=d_ff, chunk=chunk),
        out_shape=jax.ShapeDtypeStruct((t, d), F32),
        grid=(t // tm,),
        in_specs=[pl.BlockSpec((tm, d), lambda i: (i, 0)),
                  pl.BlockSpec((1, d), lambda i: (0, 0)),
                  pl.BlockSpec(w_gate_up.shape, lambda i: (0, 0)),
                  pl.BlockSpec(w_down.shape, lambda i: (0, 0))],
        out_specs=pl.BlockSpec((tm, d), lambda i: (i, 0)),
        compiler_params=_params(("parallel",), VMEM_LIMIT_MOE),
        name="dense_ffn",
    )(x, g, w_gate_up, w_down)


def _odd_prep_kernel(dq_ref, dk_ref, gz_ref, fb_ref, ca_ref, sm_ref, sp_ref,
                     q1_ref, q2_ref, kr_ref, f_ref, carry_sc):
    tm = dq_ref.shape[0]

    @pl.when(pl.program_id(1) == 0)
    def _():
        carry_sc[...] = jnp.zeros_like(carry_sc)

    cos_a = ca_ref[...]
    sin_m = sm_ref[...]
    sin_p = sp_ref[...]
    lane = lax.broadcasted_iota(jnp.int32, (tm, LANES), 1)
    first_map = lane < DIFF_QK
    for h in range(N_HEADS):
        hs = slice(h * LANES, (h + 1) * LANES)
        for src_ref, is_q in ((dq_ref, True), (dk_ref, False)):
            x = src_ref[:, hs].astype(F32)
            rot = (x * cos_a + pltpu.roll(x, LANES - ROPE_DIMS // 2, axis=1) * sin_m
                   + pltpu.roll(x, ROPE_DIMS // 2, axis=1) * sin_p)
            if is_q:
                q1_ref[:, hs] = jnp.where(first_map, rot, 0.0).astype(q1_ref.dtype)
                q2_ref[:, hs] = jnp.where(first_map, 0.0, rot).astype(q2_ref.dtype)
            else:
                kr_ref[:, hs] = rot.astype(kr_ref.dtype)

    lf = _log_sigmoid(gz_ref[...] + fb_ref[...])
    r = lax.broadcasted_iota(jnp.int32, (tm, tm), 0)
    c = lax.broadcasted_iota(jnp.int32, (tm, tm), 1)
    lower = (c <= r).astype(F32)
    fcum = jnp.dot(lower, lf, precision=HIGHEST, preferred_element_type=F32) + carry_sc[0:1, :]
    f_ref[...] = fcum * LOG2E
    carry_sc[...] = jnp.broadcast_to(fcum[tm - 1:tm, :], carry_sc.shape)


def odd_prep(z, gz, f_bias_row, cos_a, sin_m, sin_p, batch):
    t = z.shape[0]
    seq = t // batch
    tm = min(TM_PROJ, seq)
    ns = seq // tm
    w = MIX_WIDTH
    tok = lambda col: pl.BlockSpec((tm, w), lambda b, s, col=col: (b * ns + s, col))
    tab = pl.BlockSpec((tm, LANES), lambda b, s: (s, 0))
    out_tok = pl.BlockSpec((tm, w), lambda b, s: (b * ns + s, 0))
    return pl.pallas_call(
        _odd_prep_kernel,
        out_shape=(jax.ShapeDtypeStruct((t, w), BF16), jax.ShapeDtypeStruct((t, w), BF16),
                   jax.ShapeDtypeStruct((t, w), BF16), jax.ShapeDtypeStruct((t, LANES), F32)),
        grid=(batch, ns),
        in_specs=[tok(3), tok(4),
                  pl.BlockSpec((tm, LANES), lambda b, s: (b * ns + s, 0)),
                  pl.BlockSpec((1, LANES), lambda b, s: (0, 0)),
                  tab, tab, tab],
        out_specs=(out_tok, out_tok, out_tok, pl.BlockSpec((tm, LANES), lambda b, s: (b * ns + s, 0))),
        scratch_shapes=[pltpu.VMEM((8, LANES), F32)],
        compiler_params=_params(("parallel", "arbitrary")),
        name="odd_prep",
    )(z, z, gz, f_bias_row, cos_a, sin_m, sin_p)


def _softmax_chunk(q, k, v, fk, fq, m_ref, l_ref, acc_ref, mask):
    s = lax.dot_general(q, k, (((1,), (1,)), ((), ())), preferred_element_type=F32)
    if fk is not None:
        s = s - fk
    if mask is not None:
        s = jnp.where(mask, s, NEG_INF)
    row_max = jnp.max(s, axis=-1, keepdims=True)
    if fq is not None:
        row_max = row_max + fq
    m_prev = m_ref[...]
    m_new = jnp.maximum(m_prev, row_max)
    p = jnp.exp2(s - (m_new if fq is None else m_new - fq))
    alpha = jnp.exp2(m_prev - m_new)
    l_ref[...] = alpha * l_ref[...] + jnp.sum(p, axis=-1, keepdims=True)
    acc_ref[...] = alpha * acc_ref[...] + jnp.dot(p.astype(BF16), v, preferred_element_type=F32)
    m_ref[...] = m_new


def _causal_sweep(i, tq, ch, streams):
    for st in streams:
        st["m"][...] = jnp.full_like(st["m"], -1e30)
        st["l"][...] = jnp.zeros_like(st["l"])
        st["acc"][...] = jnp.zeros_like(st["acc"])

    def chunk(j, mask):
        start = pl.multiple_of(j * ch, ch)
        for st in streams:
            fk = st["fk"](j) if st["fk"] is not None else None
            _softmax_chunk(st["q"], st["k"](start), st["v"](start), fk, st["fq"], st["m"], st["l"], st["acc"], mask)

    def body(j, carry):
        chunk(j, None)
        return carry

    n_full = (i * tq) // ch
    lax.fori_loop(0, n_full, body, 0)
    r = lax.broadcasted_iota(jnp.int32, (tq, ch), 0)
    c = lax.broadcasted_iota(jnp.int32, (tq, ch), 1)
    chunk(n_full, (c - r) <= (i * tq - n_full * ch))


def _fox_attention_kernel(q_ref, k_ref, v_ref, fcol_ref, frow_ref, out_ref, *scratch, tq, ch):
    hp = pl.program_id(1)
    i = pl.program_id(2)
    lane = lax.broadcasted_iota(jnp.int32, (tq, LANES), 1)
    fcol = fcol_ref[...]
    streams = []
    for n in range(2):
        hs = slice(n * LANES, (n + 1) * LANES)
        streams.append(dict(
            q=q_ref[:, hs],
            k=lambda start, hs=hs: k_ref[pl.ds(start, ch), hs],
            v=lambda start, hs=hs: v_ref[pl.ds(start, ch), hs],
            fk=lambda j, n=n: frow_ref[n, j],
            fq=jnp.sum(jnp.where(lane == 2 * hp + n, fcol, 0.0), axis=-1, keepdims=True),
            m=scratch[3 * n], l=scratch[3 * n + 1], acc=scratch[3 * n + 2]))
    _causal_sweep(i, tq, ch, streams)
    for n, st in enumerate(streams):
        out_ref[:, n * LANES:(n + 1) * LANES] = (st["acc"][...] / st["l"][...]).astype(out_ref.dtype)


def _diff_attention_kernel(q1_ref, q2_ref, k_ref, v_ref, lam_ref, ng_ref, out_ref, *scratch, tq, ch, lambda_init):
    i = pl.program_id(2)
    streams = []
    for n, q_ref in enumerate((q1_ref, q2_ref)):
        streams.append(dict(
            q=q_ref[...],
            k=lambda start: k_ref[pl.ds(start, ch), :],
            v=lambda start: v_ref[pl.ds(start, ch), :],
            fk=None, fq=None,
            m=scratch[3 * n], l=scratch[3 * n + 1], acc=scratch[3 * n + 2]))
    _causal_sweep(i, tq, ch, streams)
    o1, o2 = (st["acc"][...] / st["l"][...] for st in streams)
    lam_p = lam_ref[...]
    lam = (jnp.exp(jnp.sum(lam_p[0:1, :] * lam_p[1:2, :], axis=-1, keepdims=True))
           - jnp.exp(jnp.sum(lam_p[2:3, :] * lam_p[3:4, :], axis=-1, keepdims=True)) + lambda_init)
    out_ref[...] = (_rms(o1 - lam * o2, ng_ref[...]) * (1.0 - lambda_init)).astype(out_ref.dtype)


def _attention_tiles(seq):
    tq = min(ATT_TQ, seq)
    ch = min(ATT_CHUNK, seq)
    assert seq % ch == 0 and ch % tq == 0
    return tq, ch


def _stream_scratch(tq):
    return [pltpu.VMEM((tq, 1), F32), pltpu.VMEM((tq, 1), F32), pltpu.VMEM((tq, LANES), F32)] * 2


def fox_attention(z, fcol, frow, batch):
    t = z.shape[0]
    seq = t // batch
    tq, ch = _attention_tiles(seq)
    nq = seq // tq
    pairs = N_HEADS // 2
    w2 = 2 * LANES
    return pl.pallas_call(
        functools.partial(_fox_attention_kernel, tq=tq, ch=ch),
        out_shape=jax.ShapeDtypeStruct((t, MIX_WIDTH), BF16),
        grid=(batch, pairs, nq),
        in_specs=[pl.BlockSpec((tq, w2), lambda b, hp, i: (b * nq + i, hp)),
                  pl.BlockSpec((seq, w2), lambda b, hp, i: (b, pairs + hp)),
                  pl.BlockSpec((seq, w2), lambda b, hp, i: (b, 2 * pairs + hp)),
                  pl.BlockSpec((tq, LANES), lambda b, hp, i: (b * nq + i, 0)),
                  pl.BlockSpec((2, seq // ch, 1, ch), lambda b, hp, i: (b * pairs + hp, 0, 0, 0))],
        out_specs=pl.BlockSpec((tq, w2), lambda b, hp, i: (b * nq + i, hp)),
        scratch_shapes=_stream_scratch(tq),
        compiler_params=_params(("parallel", "parallel", "arbitrary")),
        name="fox_attention",
    )(z, z, z, fcol, frow)


def diff_attention(q1, q2, kr, z, v_col0, lam_p, ng, lambda_init, batch):
    t = z.shape[0]
    seq = t // batch
    tq, ch = _attention_tiles(seq)
    nq = seq // tq
    qspec = pl.BlockSpec((tq, LANES), lambda b, h, i: (b * nq + i, h))
    return pl.pallas_call(
        functools.partial(_diff_attention_kernel, tq=tq, ch=ch, lambda_init=lambda_init),
        out_shape=jax.ShapeDtypeStruct((t, MIX_WIDTH), BF16),
        grid=(batch, N_HEADS, nq),
        in_specs=[qspec, qspec,
                  pl.BlockSpec((seq, LANES), lambda b, h, i: (b, h)),
                  pl.BlockSpec((seq, LANES), lambda b, h, i: (b, v_col0 + h)),
                  pl.BlockSpec(lam_p.shape, lambda b, h, i: (0, 0)),
                  pl.BlockSpec(ng.shape, lambda b, h, i: (0, 0))],
        out_specs=qspec,
        scratch_shapes=_stream_scratch(tq),
        compiler_params=_params(("parallel", "parallel", "arbitrary")),
        name="diff_attention",
    )(q1, q2, kr, z, lam_p, ng)


def _router_kernel(x_ref, g_ref, wr_ref, hp_ref, meta_ref, cnt_ref, carry_sc):
    tm = x_ref.shape[0]

    @pl.when(pl.program_id(0) == 0)
    def _():
        carry_sc[...] = jnp.zeros_like(carry_sc)

    hn = _rms(x_ref[...], g_ref[...])
    hp_ref[...] = hn
    lane = lax.broadcasted_iota(jnp.int32, (tm, LANES), 1)
    logits = jnp.dot(hn, wr_ref[...], precision=HIGHEST, preferred_element_type=F32)
    logits = jnp.where(lane < N_EXPERTS, logits, NEG_INF)
    m1 = jnp.max(logits, axis=-1, keepdims=True)
    i1 = jnp.min(jnp.where(logits == m1, lane, LANES), axis=-1, keepdims=True)
    rest = jnp.where(lane == i1, NEG_INF, logits)
    m2 = jnp.max(rest, axis=-1, keepdims=True)
    i2 = jnp.min(jnp.where(rest == m2, lane, LANES), axis=-1, keepdims=True)
    e2 = jnp.exp(m2 - m1)
    w1 = 1.0 / (1.0 + e2)
    w2 = e2 * w1

    sel1 = lane == i1
    sel2 = lane == i2
    onehot = jnp.where(sel1, 1.0, 0.0) + jnp.where(sel2, 1.0, 0.0)
    r = lax.broadcasted_iota(jnp.int32, (tm, tm), 0)
    c = lax.broadcasted_iota(jnp.int32, (tm, tm), 1)
    lower = (c <= r).astype(BF16)
    incl = jnp.dot(lower, onehot.astype(BF16), preferred_element_type=F32)
    before = incl - onehot + carry_sc[0:1, :]
    r1 = jnp.sum(jnp.where(sel1, before, 0.0), axis=-1, keepdims=True)
    r2 = jnp.sum(jnp.where(sel2, before, 0.0), axis=-1, keepdims=True)
    total = incl[tm - 1:tm, :] + carry_sc[0:1, :]
    carry_sc[...] = jnp.broadcast_to(total, carry_sc.shape)
    cnt_ref[...] = jnp.broadcast_to(total, cnt_ref.shape)

    meta = jnp.where(lane == 0, i1.astype(F32), 0.0)
    meta = jnp.where(lane == 1, i2.astype(F32), meta)
    meta = jnp.where(lane == 2, r1, meta)
    meta = jnp.where(lane == 3, r2, meta)
    meta = jnp.where(lane == 4, w1, meta)
    meta = jnp.where(lane == 5, w2, meta)
    meta_ref[...] = meta


def moe_router(x, g, w_router_pad):
    t, d = x.shape
    tm = min(TM_PROJ, t)
    return pl.pallas_call(
        _router_kernel,
        out_shape=(jax.ShapeDtypeStruct((t, d), F32), jax.ShapeDtypeStruct((t, LANES), F32),
                   jax.ShapeDtypeStruct((8, LANES), F32)),
        grid=(t // tm,),
        in_specs=[pl.BlockSpec((tm, d), lambda i: (i, 0)),
                  pl.BlockSpec((1, d), lambda i: (0, 0)),
                  pl.BlockSpec((d, LANES), lambda i: (0, 0))],
        out_specs=(pl.BlockSpec((tm, d), lambda i: (i, 0)), pl.BlockSpec((tm, LANES), lambda i: (i, 0)),
                   pl.BlockSpec((8, LANES), lambda i: (0, 0))),
        scratch_shapes=[pltpu.VMEM((8, LANES), F32)],
        compiler_params=_params(("arbitrary",)),
        name="moe_router",
    )(x, g, w_router_pad)


def _expert_ffn_kernel(te_ref, nt_ref, nv_ref, tok_ref, dst_ref, hn_ref, pad_in_ref, wgu_ref, wd_ref,
                       planes_ref, pad_ref, xbuf, ybuf, gsem, ssem, *, d_ff, chunk, tm):
    del te_ref, pad_in_ref
    i = pl.program_id(0)
    nt = nt_ref[0]
    slot = i % 2

    def start_gather(tile, s):
        base = tile * tm

        def body(r, carry):
            pltpu.make_async_copy(hn_ref.at[pl.ds(tok_ref[base + r], 1)], xbuf.at[s, pl.ds(r, 1)], gsem.at[s]).start()
            return carry

        lax.fori_loop(0, tm, body, 0, unroll=8)

    def start_scatter(tile, s):
        base = tile * tm

        def body(out_ref, r, carry):
            pltpu.make_async_copy(ybuf.at[s, pl.ds(r, 1)], out_ref.at[pl.ds(dst_ref[base + r], 1)], ssem.at[s]).start()
            return carry

        lax.fori_loop(0, nv_ref[tile], functools.partial(body, planes_ref), 0)
        lax.fori_loop(nv_ref[tile], tm, functools.partial(body, pad_ref), 0)

    def wait_gather(s):
        pltpu.make_async_copy(hn_ref.at[pl.ds(0, tm)], xbuf.at[s], gsem.at[s]).wait()

    def wait_scatter(s):
        pltpu.make_async_copy(ybuf.at[s], planes_ref.at[pl.ds(0, tm)], ssem.at[s]).wait()

    @pl.when(i == 0)
    def _():
        start_gather(0, 0)

    @pl.when(i + 1 < nt)
    def _():
        start_gather(i + 1, 1 - slot)

    @pl.when(i < nt)
    def _():
        wait_gather(slot)

        @pl.when(i >= 2)
        def _():
            wait_scatter(slot)

        ybuf[slot] = _swiglu_accumulate(xbuf[slot].astype(BF16), wgu_ref.at[0], wd_ref.at[0], d_ff, chunk)
        start_scatter(i, slot)

    @pl.when(i == nt - 1)
    def _():
        wait_scatter(slot)

        @pl.when(i >= 1)
        def _():
            wait_scatter(1 - slot)


def moe_expert_ffn(tile_expert, n_tiles, tile_valid, row_token, row_dst, hn, w_gate_up, w_down):
    t, d = hn.shape
    n_exp, d_ff = w_down.shape[:2]
    tm = min(TM_MOE, 2 * t)
    n_grid = row_token.shape[0] // tm
    chunk = FF_CHUNK_MOE if d_ff % FF_CHUNK_MOE == 0 else d_ff
    w_index = lambda i, te, nt, nv, tok, dst: (te[i], 0, 0)
    planes, _ = pl.pallas_call(
        functools.partial(_expert_ffn_kernel, d_ff=d_ff, chunk=chunk, tm=tm),
        out_shape=(jax.ShapeDtypeStruct((2 * t, d), F32), jax.ShapeDtypeStruct((n_exp * tm, d), F32)),
        grid_spec=pltpu.PrefetchScalarGridSpec(
            num_scalar_prefetch=5,
            grid=(n_grid,),
            in_specs=[pl.BlockSpec(memory_space=pl.ANY), pl.BlockSpec(memory_space=pl.ANY),
                      pl.BlockSpec((1, d, 2 * d_ff), w_index),
                      pl.BlockSpec((1, d_ff, d), w_index)],
            out_specs=(pl.BlockSpec(memory_space=pl.ANY), pl.BlockSpec(memory_space=pl.ANY)),
            scratch_shapes=[pltpu.VMEM((2, tm, d), F32), pltpu.VMEM((2, tm, d), F32),
                            pltpu.SemaphoreType.DMA((2,)), pltpu.SemaphoreType.DMA((2,))]),
        input_output_aliases={6: 1},
        compiler_params=_params(("arbitrary",), VMEM_LIMIT_MOE),
        name="moe_expert_ffn",
    )(tile_expert, n_tiles, tile_valid, row_token, row_dst, hn, jnp.zeros((n_exp * tm, d), F32), w_gate_up, w_down)
    return planes


def _combine_kernel(x_ref, meta_ref, g_ref, p0_ref, p1_ref, out_ref):
    meta = meta_ref[...]
    y = x_ref[...] + meta[:, 4:5] * p0_ref[...] + meta[:, 5:6] * p1_ref[...]
    out_ref[...] = _rms(y, g_ref[...])


def moe_combine(x, meta, g_final, planes):
    t, d = x.shape
    tm = min(TM_PROJ, t)
    nb = t // tm
    return pl.pallas_call(
        _combine_kernel,
        out_shape=jax.ShapeDtypeStruct((t, d), F32),
        grid=(nb,),
        in_specs=[pl.BlockSpec((tm, d), lambda i: (i, 0)),
                  pl.BlockSpec((tm, LANES), lambda i: (i, 0)),
                  pl.BlockSpec((1, d), lambda i: (0, 0)),
                  pl.BlockSpec((tm, d), lambda i: (i, 0)),
                  pl.BlockSpec((tm, d), lambda i: (nb + i, 0))],
        out_specs=pl.BlockSpec((tm, d), lambda i: (i, 0)),
        compiler_params=_params(("parallel",)),
        name="moe_combine",
    )(x, meta, g_final, planes, planes)


def moe_layer(x, g_ffn, w_router, w_gate_up, w_down, g_final):
    t, d = x.shape
    n_exp = w_router.shape[1]
    tm = min(TM_MOE, 2 * t)
    n_rows = 2 * t + n_exp * tm
    wr_pad = jnp.zeros((d, LANES), F32).at[:, :n_exp].set(w_router)
    hn, meta, counts = moe_router(x, g_ffn, wr_pad)

    cnt = counts[0, :n_exp].astype(jnp.int32)
    padded = ((cnt + tm - 1) // tm) * tm
    ends = jnp.cumsum(padded)
    starts = ends - padded
    e_idx = meta[:, 0:2].astype(jnp.int32)
    rank = meta[:, 2:4].astype(jnp.int32)
    start_of = jnp.sum(jnp.where(e_idx[..., None] == jnp.arange(n_exp), starts, 0), axis=-1)
    pos = (start_of + rank).T.reshape(-1)
    tile_start = jnp.arange(n_rows // tm, dtype=jnp.int32) * tm
    tile_expert = jnp.minimum(jnp.sum(tile_start[:, None] >= ends[None, :], axis=-1), n_exp - 1).astype(jnp.int32)
    n_tiles = (ends[-1:] // tm).astype(jnp.int32)
    tile_valid = jnp.clip(cnt[tile_expert] - (tile_start - starts[tile_expert]), 0, tm).astype(jnp.int32)
    pad_dst = (tile_expert[:, None] * tm + jnp.arange(tm, dtype=jnp.int32)[None, :]).reshape(-1)
    served = jnp.full((n_rows,), -1, jnp.int32).at[pos].set(jnp.arange(2 * t, dtype=jnp.int32), unique_indices=True)
    row_dst = jnp.where(served >= 0, served, pad_dst)
    row_token = jnp.where(served >= 0, served % t, 0)

    planes = moe_expert_ffn(tile_expert, n_tiles, tile_valid, row_token, row_dst, hn, w_gate_up, w_down)
    return moe_combine(x, meta, g_final, planes)


def _rope_tables(seq):
    half = ROPE_DIMS // 2
    inv = ROPE_THETA ** (-jnp.arange(half, dtype=F32) / half)
    ang = jnp.arange(seq, dtype=F32)[:, None] * inv[None, :]
    cos, sin = jnp.cos(ang), jnp.sin(ang)
    ones = jnp.ones((seq, DIFF_QK - ROPE_DIMS), F32)
    zeros = jnp.zeros((seq, DIFF_QK - ROPE_DIMS), F32)
    zh = jnp.zeros((seq, half), F32)
    cos_a = jnp.concatenate([cos, cos, ones], axis=1)
    sin_m = jnp.concatenate([-sin, zh, zeros], axis=1)
    sin_p = jnp.concatenate([zh, sin, zeros], axis=1)
    tile2 = lambda a: jnp.concatenate([a, a], axis=1)
    return tile2(cos_a), tile2(sin_m), tile2(sin_p)


def even_layer(x, batch, norm_mix, w_in, conv_w, conv_b, gate_b, ml_norm_g, sg_norm_g, sg_w, sg_b, w_out,
               norm_ffn, w_gate_up, w_down):
    w = MIX_WIDTH
    n_gate = 2 * N_HEADS
    w_main = jnp.concatenate([w_in[:, :4 * w], w_in[:, 4 * w + n_gate:]], axis=1).astype(BF16)
    w_gate = jnp.zeros((w_in.shape[0], LANES), F32).at[:, :n_gate].set(w_in[:, 4 * w:4 * w + n_gate]).astype(BF16)
    z, gz = norm_proj(x, norm_mix[None, :], w_main, w_gate)
    gates_t = gz[:, :n_gate].T
    y = even_mixer_core(z, gates_t, gate_b.reshape(n_gate, 1), conv_w, conv_b[None, :], ml_norm_g[None, :],
                        sg_norm_g[None, :], sg_w, sg_b.T, batch)
    x = proj_residual([y], [w_out.astype(BF16)], x)
    return dense_ffn(x, norm_ffn[None, :], w_gate_up.astype(BF16), w_down.astype(BF16))


def odd_mixer_layer(x, batch, norm_mix, w_in, fox_f_b, diff_lambda, diff_norm_g, w_out, lambda_init):
    w = MIX_WIDTH
    t = x.shape[0]
    seq = t // batch
    w_main = jnp.concatenate([w_in[:, :w] * (LOG2E * HEAD_DIM ** -0.5), w_in[:, w:3 * w],
                              w_in[:, 3 * w + N_HEADS:4 * w + N_HEADS] * (LOG2E * DIFF_QK ** -0.5),
                              w_in[:, 4 * w + N_HEADS:]], axis=1).astype(BF16)
    w_gate = jnp.zeros((w_in.shape[0], LANES), F32).at[:, :N_HEADS].set(w_in[:, 3 * w:3 * w + N_HEADS]).astype(BF16)
    z, gz = norm_proj(x, norm_mix[None, :], w_main, w_gate)
    f_bias_row = jnp.zeros((1, LANES), F32).at[0, :N_HEADS].set(fox_f_b)
    cos_a, sin_m, sin_p = _rope_tables(seq)
    q1, q2, kr, fcol = odd_prep(z, gz, f_bias_row, cos_a, sin_m, sin_p, batch)
    _, ch = _attention_tiles(seq)
    frow = fcol[:, :N_HEADS].reshape(batch, seq, N_HEADS).transpose(0, 2, 1).reshape(
        batch * N_HEADS, seq // ch, 1, ch)
    y_fox = fox_attention(z, fcol, frow, batch)
    y_diff = diff_attention(q1, q2, kr, z, 5 * N_HEADS, diff_lambda, diff_norm_g[None, :], lambda_init, batch)
    w_out16 = w_out.astype(BF16)
    return proj_residual([y_fox, y_diff], [w_out16[:w], w_out16[w:]], x)


def kernel(x, even_norm_mix, even_w_in, even_ml_conv_w, even_ml_conv_b, even_ml_gate_b, even_ml_norm_g,
           even_sg_norm_g, even_sg_w, even_sg_b, even_w_out, even_norm_ffn, ffn_w_gate_up, ffn_w_down,
           odd_norm_mix, odd_w_in, odd_fox_f_b, odd_diff_lambda, odd_diff_norm_g, odd_w_out, odd_norm_ffn,
           moe_w_router, moe_w_gate_up, moe_w_down, final_norm):
    batch, seq, d = x.shape
    h = x.reshape(batch * seq, d)
    h = even_layer(h, batch, even_norm_mix[0], even_w_in[0], even_ml_conv_w[0], even_ml_conv_b[0],
                   even_ml_gate_b[0], even_ml_norm_g[0], even_sg_norm_g[0], even_sg_w[0], even_sg_b[0],
                   even_w_out[0], even_norm_ffn[0], ffn_w_gate_up[0], ffn_w_down[0])
    lambda_init = 0.8 - 0.6 * math.exp(-0.3 * 1)
    h = odd_mixer_layer(h, batch, odd_norm_mix[0], odd_w_in[0], odd_fox_f_b[0], odd_diff_lambda[0],
                        odd_diff_norm_g[0], odd_w_out[0], lambda_init)
    out = moe_layer(h, odd_norm_ffn[0][None, :], moe_w_router[0], moe_w_gate_up[0].astype(BF16),
                    moe_w_down[0].astype(BF16), final_norm[None, :])
    return out.reshape(batch, seq, d)
```

```python
import functools
import math

import jax
import jax.numpy as jnp
from jax import lax
from jax.experimental import pallas as pl
from jax.experimental.pallas import tpu as pltpu

F32 = jnp.float32
BF16 = jnp.bfloat16
HIGHEST = lax.Precision.HIGHEST

EPS = 1e-6
CHUNK = 128
HEAD_DIM = 128
N_HEADS = 4
MIX_WIDTH = N_HEADS * HEAD_DIM
CONV_TAPS = 4
DIFF_QK = 64
ROPE_DIMS = DIFF_QK // 4
ROPE_THETA = 500000.0
N_EXPERTS = 8
LANES = 128
NEG_INF = float("-inf")

VMEM_LIMIT_DEFAULT = 48 * 1024 * 1024
VMEM_LIMIT_MOE = 60 * 1024 * 1024

TM_PROJ = 512
TM_MOE = 512
FF_CHUNK_DENSE = 256
FF_CHUNK_MOE = 512
ATT_TQ = 512
ATT_CHUNK = 1024
LOG2E = math.log2(math.e)


def _params(semantics, vmem=VMEM_LIMIT_DEFAULT):
    return pltpu.CompilerParams(dimension_semantics=semantics, vmem_limit_bytes=vmem)


def _rms(x, g):
    return x * lax.rsqrt(jnp.mean(x * x, axis=-1, keepdims=True) + EPS) * g


def _sigmoid(x):
    return 1.0 / (1.0 + jnp.exp(-x))


def _log_sigmoid(x):
    return jnp.minimum(x, 0.0) - jnp.log1p(jnp.exp(-jnp.abs(x)))


def _gelu_tanh(x):
    return 0.5 * x * (1.0 + jnp.tanh(math.sqrt(2.0 / math.pi) * (x + 0.044715 * (x * x * x))))


def _norm_proj_kernel(x_ref, g_ref, w_ref, wg_ref, z_ref, gz_ref, *, col_chunk):
    hn = _rms(x_ref[...], g_ref[...]).astype(BF16)
    for c in range(w_ref.shape[1] // col_chunk):
        cols = slice(c * col_chunk, (c + 1) * col_chunk)
        z_ref[:, cols] = jnp.dot(hn, w_ref[:, cols], preferred_element_type=F32).astype(z_ref.dtype)
    gz_ref[...] = jnp.dot(hn, wg_ref[...], preferred_element_type=F32)


def norm_proj(x, g, w_main, w_gate):
    t, d = x.shape
    n = w_main.shape[1]
    tm = min(TM_PROJ, t)
    return pl.pallas_call(
        functools.partial(_norm_proj_kernel, col_chunk=512),
        out_shape=(jax.ShapeDtypeStruct((t, n), BF16), jax.ShapeDtypeStruct((t, LANES), F32)),
        grid=(t // tm,),
        in_specs=[
            pl.BlockSpec((tm, d), lambda i: (i, 0)),
            pl.BlockSpec((1, d), lambda i: (0, 0)),
            pl.BlockSpec((d, n), lambda i: (0, 0)),
            pl.BlockSpec((d, LANES), lambda i: (0, 0)),
        ],
        out_specs=(pl.BlockSpec((tm, n), lambda i: (i, 0)), pl.BlockSpec((tm, LANES), lambda i: (i, 0))),
        compiler_params=_params(("parallel",)),
        name="norm_proj",
    )(x, g, w_main, w_gate)


def _even_mixer_kernel(zq_ref, zk_ref, zv_ref, zo_ref, zu_ref, zs_ref, gt_ref, gb_ref, cw_ref, cb_ref,
                       mlg_ref, sgg_ref, sgw_ref, sgb_ref, y_ref,
                       c_sc, n_sc, m_sc, tail_sc):
    L = CHUNK

    @pl.when(pl.program_id(1) == 0)
    def _():
        c_sc[...] = jnp.zeros_like(c_sc)
        n_sc[...] = jnp.zeros_like(n_sc)
        m_sc[...] = jnp.zeros_like(m_sc)
        tail_sc[...] = jnp.zeros_like(tail_sc)

    qk_raw = jnp.concatenate([zq_ref[...], zk_ref[...]], axis=1).astype(F32)
    ext = jnp.concatenate([tail_sc[...], qk_raw], axis=0)
    conv = cb_ref[...] + cw_ref[CONV_TAPS - 1:CONV_TAPS, :] * qk_raw
    for j in range(CONV_TAPS - 1):
        lo = 8 - (CONV_TAPS - 1) + j
        conv = conv + cw_ref[j:j + 1, :] * ext[lo:lo + L, :]
    tail_sc[...] = qk_raw[L - 8:, :]
    qk = conv * _sigmoid(conv)
    q_all = qk[:, :MIX_WIDTH] * (HEAD_DIM ** -0.5)
    k_all = qk[:, MIX_WIDTH:]

    gates = gt_ref[...] + gb_ref[...]
    row = lax.broadcasted_iota(jnp.int32, (8, L), 0)
    lgate = jnp.where(row < N_HEADS, gates, _log_sigmoid(gates))
    src = lax.broadcasted_iota(jnp.int32, (L, L), 0)
    dst = lax.broadcasted_iota(jnp.int32, (L, L), 1)
    upper = (src <= dst).astype(F32)
    csum = jnp.dot(lgate, upper, precision=HIGHEST, preferred_element_type=F32)
    rows8 = jnp.where(row < N_HEADS, gates, csum)
    cols = jnp.concatenate([rows8, jnp.zeros((L - 8, L), F32)], axis=0).T
    causal = dst <= src

    for h in range(N_HEADS):
        hs = slice(h * HEAD_DIM, (h + 1) * HEAD_DIM)
        li_row = rows8[h:h + 1, :]
        b_row = rows8[N_HEADS + h:N_HEADS + h + 1, :]
        li_col = cols[:, h:h + 1]
        b_col = cols[:, N_HEADS + h:N_HEADS + h + 1]
        m_prev = m_sc[h][0:1, 0:1]
        c_prev = c_sc[h]
        n_prev = n_sc[h][0:1, :]

        qh = q_all[:, hs]
        kh = k_all[:, hs]
        vh = zv_ref[:, hs]
        qh16 = qh.astype(BF16)
        kh16 = kh.astype(BF16)

        dmat = jnp.where(causal, b_col - b_row + li_row, NEG_INF)
        inter = b_col + m_prev
        m_row = jnp.maximum(inter, jnp.max(dmat, axis=-1, keepdims=True))
        w_intra = jnp.exp(dmat - m_row)
        w_inter = jnp.exp(inter - m_row)
        s = lax.dot_general(qh16, kh16, (((1,), (1,)), ((), ())), preferred_element_type=F32) * w_intra
        cq = lax.dot_general(qh16, c_prev.astype(BF16), (((1,), (1,)), ((), ())), preferred_element_type=F32)
        num = jnp.dot(s.astype(BF16), vh, preferred_element_type=F32) + w_inter * cq
        den = jnp.sum(s, axis=-1, keepdims=True) + w_inter * jnp.sum(qh * n_prev, axis=-1, keepdims=True)
        hm = num / jnp.maximum(jnp.abs(den), jnp.exp(-m_row))

        b_last = b_row[:, L - 1:L]
        g_row = b_last - b_row + li_row
        m_new = jnp.maximum(b_last + m_prev, jnp.max(g_row, axis=-1, keepdims=True))
        w_col = jnp.exp(b_last - b_col + li_col - m_new)
        decay = jnp.exp(b_last + m_prev - m_new)
        vw = (vh.astype(F32) * w_col).astype(BF16)
        c_sc[h] = decay * c_prev + lax.dot_general(vw, kh16, (((0,), (0,)), ((), ())), preferred_element_type=F32)
        n_new = decay * n_prev + jnp.sum(kh * w_col, axis=0, keepdims=True)
        n_sc[h] = jnp.broadcast_to(n_new, (8, HEAD_DIM))
        m_sc[h] = jnp.broadcast_to(m_new, (8, LANES))

        hm = _rms(hm, mlg_ref[:, hs])
        y_ref[:, hs] = (_sigmoid(zo_ref[:, hs].astype(F32)) * hm).astype(y_ref.dtype)

    u = _gelu_tanh(zu_ref[...].astype(F32))
    vs = _rms(_gelu_tanh(zs_ref[...].astype(F32)), sgg_ref[...]).astype(BF16)
    for g in range(N_HEADS):
        gs = slice(g * HEAD_DIM, (g + 1) * HEAD_DIM)
        wg = jnp.where(causal, sgw_ref[g], 0.0).astype(BF16)
        mixed = jnp.dot(wg, vs[:, gs], preferred_element_type=F32) + sgb_ref[:, g:g + 1]
        y_ref[:, MIX_WIDTH + g * HEAD_DIM:MIX_WIDTH + (g + 1) * HEAD_DIM] = (u[:, gs] * mixed).astype(y_ref.dtype)


def even_mixer_core(z, gates_t, gate_bias_col, conv_w, conv_b, ml_norm_g, sg_norm_g, sg_w, sg_b_t, batch):
    t = z.shape[0]
    nc = t // batch // CHUNK
    w = MIX_WIDTH

    def zspec(col):
        return pl.BlockSpec((CHUNK, w), lambda b, c, col=col: (b * nc + c, col))

    full = lambda shape: pl.BlockSpec(shape, lambda b, c: (0,) * len(shape))
    return pl.pallas_call(
        _even_mixer_kernel,
        out_shape=jax.ShapeDtypeStruct((t, 2 * w), BF16),
        grid=(batch, nc),
        in_specs=[zspec(0), zspec(1), zspec(2), zspec(3), zspec(4), zspec(5),
                  pl.BlockSpec((8, CHUNK), lambda b, c: (0, b * nc + c)),
                  full((8, 1)), full((CONV_TAPS, 2 * w)), full((1, 2 * w)),
                  full((1, w)), full((1, w)), full((N_HEADS, CHUNK, CHUNK)), full((CHUNK, N_HEADS))],
        out_specs=pl.BlockSpec((CHUNK, 2 * w), lambda b, c: (b * nc + c, 0)),
        scratch_shapes=[pltpu.VMEM((N_HEADS, HEAD_DIM, HEAD_DIM), F32),
                        pltpu.VMEM((N_HEADS, 8, HEAD_DIM), F32),
                        pltpu.VMEM((N_HEADS, 8, LANES), F32),
                        pltpu.VMEM((8, 2 * w), F32)],
        compiler_params=_params(("parallel", "arbitrary")),
        name="even_mixer",
    )(z, z, z, z, z, z, gates_t, gate_bias_col, conv_w, conv_b, ml_norm_g, sg_norm_g, sg_w, sg_b_t)


def _proj_residual_kernel(*refs, n_in):
    res_ref = refs[2 * n_in]
    out_ref = refs[2 * n_in + 1]
    acc = res_ref[...]
    for a_ref, w_ref in zip(refs[:n_in], refs[n_in:2 * n_in]):
        acc = acc + jnp.dot(a_ref[...], w_ref[...], preferred_element_type=F32)
    out_ref[...] = acc


def proj_residual(acts, weights, res):
    t, d = res.shape
    tm = min(TM_PROJ, t)
    n_in = len(acts)
    in_specs = [pl.BlockSpec((tm, a.shape[1]), lambda i: (i, 0)) for a in acts]
    in_specs += [pl.BlockSpec(w.shape, lambda i: (0, 0)) for w in weights]
    in_specs += [pl.BlockSpec((tm, d), lambda i: (i, 0))]
    return pl.pallas_call(
        functools.partial(_proj_residual_kernel, n_in=n_in),
        out_shape=jax.ShapeDtypeStruct((t, d), F32),
        grid=(t // tm,),
        in_specs=in_specs,
        out_specs=pl.BlockSpec((tm, d), lambda i: (i, 0)),
        compiler_params=_params(("parallel",)),
        name="proj_residual",
    )(*acts, *weights, res)


def _swiglu_accumulate(hn, wgu_ref, wd_ref, d_ff, chunk, after_chunk=None):
    acc = None
    n_chunks = d_ff // chunk
    for c in range(n_chunks):
        g = jnp.dot(hn, wgu_ref[:, c * chunk:(c + 1) * chunk], preferred_element_type=F32)
        u = jnp.dot(hn, wgu_ref[:, d_ff + c * chunk:d_ff + (c + 1) * chunk], preferred_element_type=F32)
        a = (g * _sigmoid(g) * u).astype(BF16)
        part = jnp.dot(a, wd_ref[c * chunk:(c + 1) * chunk, :], preferred_element_type=F32)
        acc = part if acc is None else acc + part
        if after_chunk is not None:
            after_chunk(c, n_chunks)
    return acc


def _dense_ffn_kernel(x_ref, g_ref, wgu_ref, wd_ref, out_ref, *, d_ff, chunk):
    x = x_ref[...]
    hn = _rms(x, g_ref[...]).astype(BF16)
    out_ref[...] = x + _swiglu_accumulate(hn, wgu_ref, wd_ref, d_ff, chunk)


def dense_ffn(x, g, w_gate_up, w_down):
    t, d = x.shape
    d_ff = w_down.shape[0]
    tm = min(TM_PROJ, t)
    chunk = FF_CHUNK_DENSE if d_ff % FF_CHUNK_DENSE == 0 else d_ff
    return pl.pallas_call(
        functools.partial(_dense_ffn_kernel, d_ff=d_ff, chunk=chunk),
        out_shape=jax.ShapeDtypeStruct((t, d), F32),
        grid=(t // tm,),
        in_specs=[pl.BlockSpec((tm, d), lambda i: (i, 0)),
                  pl.BlockSpec((1, d), lambda i: (0, 0)),
                  pl.BlockSpec(w_gate_up.shape, lambda i: (0, 0)),
                  pl.BlockSpec(w_down.shape, lambda i: (0, 0))],
        out_specs=pl.BlockSpec((tm, d), lambda i: (i, 0)),
        compiler_params=_params(("parallel",), VMEM_LIMIT_MOE),
        name="dense_ffn",
    )(x, g, w_gate_up, w_down)


def _odd_prep_kernel(dq_ref, dk_ref, gz_ref, fb_ref, ca_ref, sm_ref, sp_ref,
                     q1_ref, q2_ref, kr_ref, f_ref, carry_sc):
    tm = dq_ref.shape[0]

    @pl.when(pl.program_id(1) == 0)
    def _():
        carry_sc[...] = jnp.zeros_like(carry_sc)

    cos_a = ca_ref[...]
    sin_m = sm_ref[...]
    sin_p = sp_ref[...]
    lane = lax.broadcasted_iota(jnp.int32, (tm, LANES), 1)
    first_map = lane < DIFF_QK
    for h in range(N_HEADS):
        hs = slice(h * LANES, (h + 1) * LANES)
        for src_ref, is_q in ((dq_ref, True), (dk_ref, False)):
            x = src_ref[:, hs].astype(F32)
            rot = (x * cos_a + pltpu.roll(x, LANES - ROPE_DIMS // 2, axis=1) * sin_m
                   + pltpu.roll(x, ROPE_DIMS // 2, axis=1) * sin_p)
            if is_q:
                q1_ref[:, hs] = jnp.where(first_map, rot, 0.0).astype(q1_ref.dtype)
                q2_ref[:, hs] = jnp.where(first_map, 0.0, rot).astype(q2_ref.dtype)
            else:
                kr_ref[:, hs] = rot.astype(kr_ref.dtype)

    lf = _log_sigmoid(gz_ref[...] + fb_ref[...])
    r = lax.broadcasted_iota(jnp.int32, (tm, tm), 0)
    c = lax.broadcasted_iota(jnp.int32, (tm, tm), 1)
    lower = (c <= r).astype(F32)
    fcum = jnp.dot(lower, lf, precision=HIGHEST, preferred_element_type=F32) + carry_sc[0:1, :]
    f_ref[...] = fcum * LOG2E
    carry_sc[...] = jnp.broadcast_to(fcum[tm - 1:tm, :], carry_sc.shape)


def odd_prep(z, gz, f_bias_row, cos_a, sin_m, sin_p, batch):
    t = z.shape[0]
    seq = t // batch
    tm = min(TM_PROJ, seq)
    ns = seq // tm
    w = MIX_WIDTH
    tok = lambda col: pl.BlockSpec((tm, w), lambda b, s, col=col: (b * ns + s, col))
    tab = pl.BlockSpec((tm, LANES), lambda b, s: (s, 0))
    out_tok = pl.BlockSpec((tm, w), lambda b, s: (b * ns + s, 0))
    return pl.pallas_call(
        _odd_prep_kernel,
        out_shape=(jax.ShapeDtypeStruct((t, w), BF16), jax.ShapeDtypeStruct((t, w), BF16),
                   jax.ShapeDtypeStruct((t, w), BF16), jax.ShapeDtypeStruct((t, LANES), F32)),
        grid=(batch, ns),
        in_specs=[tok(3), tok(4),
                  pl.BlockSpec((tm, LANES), lambda b, s: (b * ns + s, 0)),
                  pl.BlockSpec((1, LANES), lambda b, s: (0, 0)),
                  tab, tab, tab],
        out_specs=(out_tok, out_tok, out_tok, pl.BlockSpec((tm, LANES), lambda b, s: (b * ns + s, 0))),
        scratch_shapes=[pltpu.VMEM((8, LANES), F32)],
        compiler_params=_params(("parallel", "arbitrary")),
        name="odd_prep",
    )(z, z, gz, f_bias_row, cos_a, sin_m, sin_p)


def _softmax_chunk(q, k, v, fk, fq, m_ref, l_ref, acc_ref, mask):
    s = lax.dot_general(q, k, (((1,), (1,)), ((), ())), preferred_element_type=F32)
    if fk is not None:
        s = s - fk
    if mask is not None:
        s = jnp.where(mask, s, NEG_INF)
    row_max = jnp.max(s, axis=-1, keepdims=True)
    if fq is not None:
        row_max = row_max + fq
    m_prev = m_ref[...]
    m_new = jnp.maximum(m_prev, row_max)
    p = jnp.exp2(s - (m_new if fq is None else m_new - fq))
    alpha = jnp.exp2(m_prev - m_new)
    l_ref[...] = alpha * l_ref[...] + jnp.sum(p, axis=-1, keepdims=True)
    acc_ref[...] = alpha * acc_ref[...] + jnp.dot(p.astype(BF16), v, preferred_element_type=F32)
    m_ref[...] = m_new


def _causal_sweep(i, tq, ch, streams):
    for st in streams:
        st["m"][...] = jnp.full_like(st["m"], -1e30)
        st["l"][...] = jnp.zeros_like(st["l"])
        st["acc"][...] = jnp.zeros_like(st["acc"])

    def chunk(j, mask):
        start = pl.multiple_of(j * ch, ch)
        for st in streams:
            fk = st["fk"](j) if st["fk"] is not None else None
            _softmax_chunk(st["q"], st["k"](start), st["v"](start), fk, st["fq"], st["m"], st["l"], st["acc"], mask)

    def body(j, carry):
        chunk(j, None)
        return carry

    n_full = (i * tq) // ch
    lax.fori_loop(0, n_full, body, 0)
    r = lax.broadcasted_iota(jnp.int32, (tq, ch), 0)
    c = lax.broadcasted_iota(jnp.int32, (tq, ch), 1)
    chunk(n_full, (c - r) <= (i * tq - n_full * ch))


def _fox_attention_kernel(q_ref, k_ref, v_ref, fcol_ref, frow_ref, out_ref, *scratch, tq, ch):
    hp = pl.program_id(1)
    i = pl.program_id(2)
    lane = lax.broadcasted_iota(jnp.int32, (tq, LANES), 1)
    fcol = fcol_ref[...]
    streams = []
    for n in range(2):
        hs = slice(n * LANES, (n + 1) * LANES)
        streams.append(dict(
            q=q_ref[:, hs],
            k=lambda start, hs=hs: k_ref[pl.ds(start, ch), hs],
            v=lambda start, hs=hs: v_ref[pl.ds(start, ch), hs],
            fk=lambda j, n=n: frow_ref[n, j],
            fq=jnp.sum(jnp.where(lane == 2 * hp + n, fcol, 0.0), axis=-1, keepdims=True),
            m=scratch[3 * n], l=scratch[3 * n + 1], acc=scratch[3 * n + 2]))
    _causal_sweep(i, tq, ch, streams)
    for n, st in enumerate(streams):
        out_ref[:, n * LANES:(n + 1) * LANES] = (st["acc"][...] / st["l"][...]).astype(out_ref.dtype)


def _diff_attention_kernel(q1_ref, q2_ref, k_ref, v_ref, lam_ref, ng_ref, out_ref, *scratch, tq, ch, lambda_init):
    i = pl.program_id(2)
    streams = []
    for n, q_ref in enumerate((q1_ref, q2_ref)):
        streams.append(dict(
            q=q_ref[...],
            k=lambda start: k_ref[pl.ds(start, ch), :],
            v=lambda start: v_ref[pl.ds(start, ch), :],
            fk=None, fq=None,
            m=scratch[3 * n], l=scratch[3 * n + 1], acc=scratch[3 * n + 2]))
    _causal_sweep(i, tq, ch, streams)
    o1, o2 = (st["acc"][...] / st["l"][...] for st in streams)
    lam_p = lam_ref[...]
    lam = (jnp.exp(jnp.sum(lam_p[0:1, :] * lam_p[1:2, :], axis=-1, keepdims=True))
           - jnp.exp(jnp.sum(lam_p[2:3, :] * lam_p[3:4, :], axis=-1, keepdims=True)) + lambda_init)
    out_ref[...] = (_rms(o1 - lam * o2, ng_ref[...]) * (1.0 - lambda_init)).astype(out_ref.dtype)


def _attention_tiles(seq):
    tq = min(ATT_TQ, seq)
    ch = min(ATT_CHUNK, seq)
    assert seq % ch == 0 and ch % tq == 0
    return tq, ch


def _stream_scratch(tq):
    return [pltpu.VMEM((tq, 1), F32), pltpu.VMEM((tq, 1), F32), pltpu.VMEM((tq, LANES), F32)] * 2


def fox_attention(z, fcol, frow, batch):
    t = z.shape[0]
    seq = t // batch
    tq, ch = _attention_tiles(seq)
    nq = seq // tq
    pairs = N_HEADS // 2
    w2 = 2 * LANES
    return pl.pallas_call(
        functools.partial(_fox_attention_kernel, tq=tq, ch=ch),
        out_shape=jax.ShapeDtypeStruct((t, MIX_WIDTH), BF16),
        grid=(batch, pairs, nq),
        in_specs=[pl.BlockSpec((tq, w2), lambda b, hp, i: (b * nq + i, hp)),
                  pl.BlockSpec((seq, w2), lambda b, hp, i: (b, pairs + hp)),
                  pl.BlockSpec((seq, w2), lambda b, hp, i: (b, 2 * pairs + hp)),
                  pl.BlockSpec((tq, LANES), lambda b, hp, i: (b * nq + i, 0)),
                  pl.BlockSpec((2, seq // ch, 1, ch), lambda b, hp, i: (b * pairs + hp, 0, 0, 0))],
        out_specs=pl.BlockSpec((tq, w2), lambda b, hp, i: (b * nq + i, hp)),
        scratch_shapes=_stream_scratch(tq),
        compiler_params=_params(("parallel", "parallel", "arbitrary")),
        name="fox_attention",
    )(z, z, z, fcol, frow)


def diff_attention(q1, q2, kr, z, v_col0, lam_p, ng, lambda_init, batch):
    t = z.shape[0]
    seq = t // batch
    tq, ch = _attention_tiles(seq)
    nq = seq // tq
    qspec = pl.BlockSpec((tq, LANES), lambda b, h, i: (b * nq + i, h))
    return pl.pallas_call(
        functools.partial(_diff_attention_kernel, tq=tq, ch=ch, lambda_init=lambda_init),
        out_shape=jax.ShapeDtypeStruct((t, MIX_WIDTH), BF16),
        grid=(batch, N_HEADS, nq),
        in_specs=[qspec, qspec,
                  pl.BlockSpec((seq, LANES), lambda b, h, i: (b, h)),
                  pl.BlockSpec((seq, LANES), lambda b, h, i: (b, v_col0 + h)),
                  pl.BlockSpec(lam_p.shape, lambda b, h, i: (0, 0)),
                  pl.BlockSpec(ng.shape, lambda b, h, i: (0, 0))],
        out_specs=qspec,
        scratch_shapes=_stream_scratch(tq),
        compiler_params=_params(("parallel", "parallel", "arbitrary")),
        name="diff_attention",
    )(q1, q2, kr, z, lam_p, ng)


def _router_kernel(x_ref, g_ref, wr_ref, hp_ref, meta_ref, cnt_ref, carry_sc):
    tm = x_ref.shape[0]

    @pl.when(pl.program_id(0) == 0)
    def _():
        carry_sc[...] = jnp.zeros_like(carry_sc)

    hn = _rms(x_ref[...], g_ref[...])
    hp_ref[...] = hn
    lane = lax.broadcasted_iota(jnp.int32, (tm, LANES), 1)
    logits = jnp.dot(hn, wr_ref[...], precision=HIGHEST, preferred_element_type=F32)
    logits = jnp.where(lane < N_EXPERTS, logits, NEG_INF)
    m1 = jnp.max(logits, axis=-1, keepdims=True)
    i1 = jnp.min(jnp.where(logits == m1, lane, LANES), axis=-1, keepdims=True)
    rest = jnp.where(lane == i1, NEG_INF, logits)
    m2 = jnp.max(rest, axis=-1, keepdims=True)
    i2 = jnp.min(jnp.where(rest == m2, lane, LANES), axis=-1, keepdims=True)
    e2 = jnp.exp(m2 - m1)
    w1 = 1.0 / (1.0 + e2)
    w2 = e2 * w1

    sel1 = lane == i1
    sel2 = lane == i2
    onehot = jnp.where(sel1, 1.0, 0.0) + jnp.where(sel2, 1.0, 0.0)
    r = lax.broadcasted_iota(jnp.int32, (tm, tm), 0)
    c = lax.broadcasted_iota(jnp.int32, (tm, tm), 1)
    lower = (c <= r).astype(BF16)
    incl = jnp.dot(lower, onehot.astype(BF16), preferred_element_type=F32)
    before = incl - onehot + carry_sc[0:1, :]
    r1 = jnp.sum(jnp.where(sel1, before, 0.0), axis=-1, keepdims=True)
    r2 = jnp.sum(jnp.where(sel2, before, 0.0), axis=-1, keepdims=True)
    total = incl[tm - 1:tm, :] + carry_sc[0:1, :]
    carry_sc[...] = jnp.broadcast_to(total, carry_sc.shape)
    cnt_ref[...] = jnp.broadcast_to(total, cnt_ref.shape)

    meta = jnp.where(lane == 0, i1.astype(F32), 0.0)
    meta = jnp.where(lane == 1, i2.astype(F32), meta)
    meta = jnp.where(lane == 2, r1, meta)
    meta = jnp.where(lane == 3, r2, meta)
    meta = jnp.where(lane == 4, w1, meta)
    meta = jnp.where(lane == 5, w2, meta)
    meta_ref[...] = meta


def moe_router(x, g, w_router_pad):
    t, d = x.shape
    tm = min(TM_PROJ, t)
    return pl.pallas_call(
        _router_kernel,
        out_shape=(jax.ShapeDtypeStruct((t, d), F32), jax.ShapeDtypeStruct((t, LANES), F32),
                   jax.ShapeDtypeStruct((8, LANES), F32)),
        grid=(t // tm,),
        in_specs=[pl.BlockSpec((tm, d), lambda i: (i, 0)),
                  pl.BlockSpec((1, d), lambda i: (0, 0)),
                  pl.BlockSpec((d, LANES), lambda i: (0, 0))],
        out_specs=(pl.BlockSpec((tm, d), lambda i: (i, 0)), pl.BlockSpec((tm, LANES), lambda i: (i, 0)),
                   pl.BlockSpec((8, LANES), lambda i: (0, 0))),
        scratch_shapes=[pltpu.VMEM((8, LANES), F32)],
        compiler_params=_params(("arbitrary",)),
        name="moe_router",
    )(x, g, w_router_pad)


def _expert_ffn_kernel(te_ref, nt_ref, tok_ref, dst_ref, hn_ref, wgu_ref, wd_ref, planes_ref,
                       xbuf, ybuf, gsem, ssem, zsem, *, d_ff, chunk, tm, n_pad_tiles):
    del te_ref
    i = pl.program_id(0)
    nt = nt_ref[0]
    slot = i % 2
    other = 1 - slot
    n_real = planes_ref.shape[0] - n_pad_tiles * tm

    def gather_row(tile, s, r):
        return pltpu.make_async_copy(hn_ref.at[pl.ds(tok_ref[tile * tm + r], 1)], xbuf.at[s, pl.ds(r, 1)], gsem.at[s])

    def scatter_row(tile, s, r):
        return pltpu.make_async_copy(ybuf.at[s, pl.ds(r, 1)], planes_ref.at[pl.ds(dst_ref[tile * tm + r], 1)],
                                     ssem.at[s])

    def start_rows(make_copy, tile, s):
        def body(r, carry):
            make_copy(tile, s, r).start()
            return carry

        lax.fori_loop(0, tm, body, 0, unroll=8)

    def wait_gather(s):
        pltpu.make_async_copy(hn_ref.at[pl.ds(0, tm)], xbuf.at[s], gsem.at[s]).wait()

    def wait_scatter(s):
        pltpu.make_async_copy(ybuf.at[s], planes_ref.at[pl.ds(0, tm)], ssem.at[s]).wait()

    def zero_fill(p):
        return pltpu.make_async_copy(ybuf.at[1], planes_ref.at[pl.ds(n_real + p * tm, tm)], zsem)

    @pl.when(i == 0)
    def _():
        start_rows(gather_row, 0, 0)
        ybuf[1] = jnp.zeros((tm, ybuf.shape[2]), F32)
        for p in range(n_pad_tiles):
            zero_fill(p).start()
        for p in range(n_pad_tiles):
            zero_fill(p).wait()

    @pl.when(i < nt)
    def _():
        wait_gather(slot)

        @pl.when(i >= 2)
        def _():
            wait_scatter(slot)

        def issue_rows(c, n_chunks):
            for r in range(c * tm // n_chunks, (c + 1) * tm // n_chunks):
                @pl.when(i + 1 < nt)
                def _():
                    gather_row(i + 1, other, r).start()

                @pl.when(i >= 1)
                def _():
                    scatter_row(i - 1, other, r).start()

        ybuf[slot] = _swiglu_accumulate(xbuf[slot].astype(BF16), wgu_ref.at[0], wd_ref.at[0], d_ff, chunk,
                                        after_chunk=issue_rows)

    @pl.when(i == nt)
    def _():
        @pl.when(i >= 2)
        def _():
            wait_scatter(slot)

        start_rows(scatter_row, i - 1, other)
        wait_scatter(other)


def moe_expert_ffn(tile_expert, n_tiles, row_token, row_dst, hn, w_gate_up, w_down):
    t, d = hn.shape
    n_exp, d_ff = w_down.shape[:2]
    tm = min(TM_MOE, 2 * t)
    n_grid = row_token.shape[0] // tm
    chunk = FF_CHUNK_MOE if d_ff % FF_CHUNK_MOE == 0 else d_ff
    w_index = lambda i, te, nt, tok, dst: (te[i], 0, 0)
    return pl.pallas_call(
        functools.partial(_expert_ffn_kernel, d_ff=d_ff, chunk=chunk, tm=tm, n_pad_tiles=n_exp),
        out_shape=jax.ShapeDtypeStruct((2 * t + n_exp * tm, d), F32),
        grid_spec=pltpu.PrefetchScalarGridSpec(
            num_scalar_prefetch=4,
            grid=(n_grid,),
            in_specs=[pl.BlockSpec(memory_space=pl.ANY),
                      pl.BlockSpec((1, d, 2 * d_ff), w_index),
                      pl.BlockSpec((1, d_ff, d), w_index)],
            out_specs=pl.BlockSpec(memory_space=pl.ANY),
            scratch_shapes=[pltpu.VMEM((2, tm, d), F32), pltpu.VMEM((2, tm, d), F32),
                            pltpu.SemaphoreType.DMA((2,)), pltpu.SemaphoreType.DMA((2,)),
                            pltpu.SemaphoreType.DMA(())]),
        compiler_params=_params(("arbitrary",), VMEM_LIMIT_MOE),
        name="moe_expert_ffn",
    )(tile_expert, n_tiles, row_token, row_dst, hn, w_gate_up, w_down)


def _combine_kernel(x_ref, meta_ref, g_ref, p0_ref, p1_ref, out_ref):
    meta = meta_ref[...]
    y = x_ref[...] + meta[:, 4:5] * p0_ref[...] + meta[:, 5:6] * p1_ref[...]
    out_ref[...] = _rms(y, g_ref[...])


def moe_combine(x, meta, g_final, planes):
    t, d = x.shape
    tm = min(TM_PROJ, t)
    nb = t // tm
    return pl.pallas_call(
        _combine_kernel,
        out_shape=jax.ShapeDtypeStruct((t, d), F32),
        grid=(nb,),
        in_specs=[pl.BlockSpec((tm, d), lambda i: (i, 0)),
                  pl.BlockSpec((tm, LANES), lambda i: (i, 0)),
                  pl.BlockSpec((1, d), lambda i: (0, 0)),
                  pl.BlockSpec((tm, d), lambda i: (i, 0)),
                  pl.BlockSpec((tm, d), lambda i: (nb + i, 0))],
        out_specs=pl.BlockSpec((tm, d), lambda i: (i, 0)),
        compiler_params=_params(("parallel",)),
        name="moe_combine",
    )(x, meta, g_final, planes, planes)


def moe_layer(x, g_ffn, w_router, w_gate_up, w_down, g_final):
    t, d = x.shape
    n_exp = w_router.shape[1]
    tm = min(TM_MOE, 2 * t)
    n_rows = 2 * t + n_exp * tm
    wr_pad = jnp.zeros((d, LANES), F32).at[:, :n_exp].set(w_router)
    hn, meta, counts = moe_router(x, g_ffn, wr_pad)

    cnt = counts[0, :n_exp].astype(jnp.int32)
    padded = ((cnt + tm - 1) // tm) * tm
    ends = jnp.cumsum(padded)
    starts = ends - padded
    e_idx = meta[:, 0:2].astype(jnp.int32)
    rank = meta[:, 2:4].astype(jnp.int32)
    start_of = jnp.sum(jnp.where(e_idx[..., None] == jnp.arange(n_exp), starts, 0), axis=-1)
    pos = (start_of + rank).T.reshape(-1)
    tile_start = jnp.arange(n_rows // tm, dtype=jnp.int32) * tm
    tile_expert = jnp.minimum(jnp.sum(tile_start[:, None] >= ends[None, :], axis=-1), n_exp - 1).astype(jnp.int32)
    n_tiles = (ends[-1:] // tm).astype(jnp.int32)
    pad_dst = (2 * t + tile_expert[:, None] * tm + jnp.arange(tm, dtype=jnp.int32)[None, :]).reshape(-1)
    served = jnp.full((n_rows,), -1, jnp.int32).at[pos].set(jnp.arange(2 * t, dtype=jnp.int32), unique_indices=True)
    row_dst = jnp.where(served >= 0, served, pad_dst)
    row_token = jnp.where(served >= 0, served % t, 0)

    planes = moe_expert_ffn(tile_expert, n_tiles, row_token, row_dst, hn, w_gate_up, w_down)
    return moe_combine(x, meta, g_final, planes)


def _rope_tables(seq):
    half = ROPE_DIMS // 2
    inv = ROPE_THETA ** (-jnp.arange(half, dtype=F32) / half)
    ang = jnp.arange(seq, dtype=F32)[:, None] * inv[None, :]
    cos, sin = jnp.cos(ang), jnp.sin(ang)
    ones = jnp.ones((seq, DIFF_QK - ROPE_DIMS), F32)
    zeros = jnp.zeros((seq, DIFF_QK - ROPE_DIMS), F32)
    zh = jnp.zeros((seq, half), F32)
    cos_a = jnp.concatenate([cos, cos, ones], axis=1)
    sin_m = jnp.concatenate([-sin, zh, zeros], axis=1)
    sin_p = jnp.concatenate([zh, sin, zeros], axis=1)
    tile2 = lambda a: jnp.concatenate([a, a], axis=1)
    return tile2(cos_a), tile2(sin_m), tile2(sin_p)


def even_layer(x, batch, norm_mix, w_in, conv_w, conv_b, gate_b, ml_norm_g, sg_norm_g, sg_w, sg_b, w_out,
               norm_ffn, w_gate_up, w_down):
    w = MIX_WIDTH
    n_gate = 2 * N_HEADS
    w_main = jnp.concatenate([w_in[:, :4 * w], w_in[:, 4 * w + n_gate:]], axis=1).astype(BF16)
    w_gate = jnp.zeros((w_in.shape[0], LANES), F32).at[:, :n_gate].set(w_in[:, 4 * w:4 * w + n_gate]).astype(BF16)
    z, gz = norm_proj(x, norm_mix[None, :], w_main, w_gate)
    gates_t = gz[:, :n_gate].T
    y = even_mixer_core(z, gates_t, gate_b.reshape(n_gate, 1), conv_w, conv_b[None, :], ml_norm_g[None, :],
                        sg_norm_g[None, :], sg_w, sg_b.T, batch)
    x = proj_residual([y], [w_out.astype(BF16)], x)
    return dense_ffn(x, norm_ffn[None, :], w_gate_up.astype(BF16), w_down.astype(BF16))


def odd_mixer_layer(x, batch, norm_mix, w_in, fox_f_b, diff_lambda, diff_norm_g, w_out, lambda_init):
    w = MIX_WIDTH
    t = x.shape[0]
    seq = t // batch
    w_main = jnp.concatenate([w_in[:, :w] * (LOG2E * HEAD_DIM ** -0.5), w_in[:, w:3 * w],
                              w_in[:, 3 * w + N_HEADS:4 * w + N_HEADS] * (LOG2E * DIFF_QK ** -0.5),
                              w_in[:, 4 * w + N_HEADS:]], axis=1).astype(BF16)
    w_gate = jnp.zeros((w_in.shape[0], LANES), F32).at[:, :N_HEADS].set(w_in[:, 3 * w:3 * w + N_HEADS]).astype(BF16)
    z, gz = norm_proj(x, norm_mix[None, :], w_main, w_gate)
    f_bias_row = jnp.zeros((1, LANES), F32).at[0, :N_HEADS].set(fox_f_b)
    cos_a, sin_m, sin_p = _rope_tables(seq)
    q1, q2, kr, fcol = odd_prep(z, gz, f_bias_row, cos_a, sin_m, sin_p, batch)
    _, ch = _attention_tiles(seq)
    frow = fcol[:, :N_HEADS].reshape(batch, seq, N_HEADS).transpose(0, 2, 1).reshape(
        batch * N_HEADS, seq // ch, 1, ch)
    y_fox = fox_attention(z, fcol, frow, batch)
    y_diff = diff_attention(q1, q2, kr, z, 5 * N_HEADS, diff_lambda, diff_norm_g[None, :], lambda_init, batch)
    w_out16 = w_out.astype(BF16)
    return proj_residual([y_fox, y_diff], [w_out16[:w], w_out16[w:]], x)


def kernel(x, even_norm_mix, even_w_in, even_ml_conv_w, even_ml_conv_b, even_ml_gate_b, even_ml_norm_g,
           even_sg_norm_g, even_sg_w, even_sg_b, even_w_out, even_norm_ffn, ffn_w_gate_up, ffn_w_down,
           odd_norm_mix, odd_w_in, odd_fox_f_b, odd_diff_lambda, odd_diff_norm_g, odd_w_out, odd_norm_ffn,
           moe_w_router, moe_w_gate_up, moe_w_down, final_norm):
    batch, seq, d = x.shape
    h = x.reshape(batch * seq, d)
    h = even_layer(h, batch, even_norm_mix[0], even_w_in[0], even_ml_conv_w[0], even_ml_conv_b[0],
                   even_ml_gate_b[0], even_ml_norm_g[0], even_sg_norm_g[0], even_sg_w[0], even_sg_b[0],
                   even_w_out[0], even_norm_ffn[0], ffn_w_gate_up[0], ffn_w_down[0])
    lambda_init = 0.8 - 0.6 * math.exp(-0.3 * 1)
    h = odd_mixer_layer(h, batch, odd_norm_mix[0], odd_w_in[0], odd_fox_f_b[0], odd_diff_lambda[0],
                        odd_diff_norm_g[0], odd_w_out[0], lambda_init)
    out = moe_layer(h, odd_norm_ffn[0][None, :], moe_w_router[0], moe_w_gate_up[0].astype(BF16),
                    moe_w_down[0].astype(BF16), final_norm[None, :])
    return out.reshape(batch, seq, d)
```

```python
import functools
import math

import jax
import jax.numpy as jnp
from jax import lax
from jax.experimental import pallas as pl
from jax.experimental.pallas import tpu as pltpu

F32 = jnp.float32
BF16 = jnp.bfloat16
HIGHEST = lax.Precision.HIGHEST

EPS = 1e-6
CHUNK = 128
HEAD_DIM = 128
N_HEADS = 4
MIX_WIDTH = N_HEADS * HEAD_DIM
CONV_TAPS = 4
DIFF_QK = 64
ROPE_DIMS = DIFF_QK // 4
ROPE_THETA = 500000.0
N_EXPERTS = 8
LANES = 128
NEG_INF = float("-inf")

VMEM_LIMIT_DEFAULT = 48 * 1024 * 1024
VMEM_LIMIT_MOE = 60 * 1024 * 1024

TM_PROJ = 512
TM_MOE = 512
FF_CHUNK_DENSE = 256
FF_CHUNK_MOE = 512
ATT_TQ = 512
ATT_CHUNK = 2048
LOG2E = math.log2(math.e)


def _params(semantics, vmem=VMEM_LIMIT_DEFAULT):
    return pltpu.CompilerParams(dimension_semantics=semantics, vmem_limit_bytes=vmem)


def _rms(x, g):
    return x * lax.rsqrt(jnp.mean(x * x, axis=-1, keepdims=True) + EPS) * g


def _sigmoid(x):
    return 1.0 / (1.0 + jnp.exp(-x))


def _log_sigmoid(x):
    return jnp.minimum(x, 0.0) - jnp.log1p(jnp.exp(-jnp.abs(x)))


def _gelu_tanh(x):
    return 0.5 * x * (1.0 + jnp.tanh(math.sqrt(2.0 / math.pi) * (x + 0.044715 * (x * x * x))))


def _norm_proj_kernel(x_ref, g_ref, w_ref, wg_ref, z_ref, gz_ref, *, col_chunk):
    hn = _rms(x_ref[...], g_ref[...]).astype(BF16)
    for c in range(w_ref.shape[1] // col_chunk):
        cols = slice(c * col_chunk, (c + 1) * col_chunk)
        z_ref[:, cols] = jnp.dot(hn, w_ref[:, cols], preferred_element_type=F32).astype(z_ref.dtype)
    gz_ref[...] = jnp.dot(hn, wg_ref[...], preferred_element_type=F32)


def norm_proj(x, g, w_main, w_gate):
    t, d = x.shape
    n = w_main.shape[1]
    tm = min(TM_PROJ, t)
    return pl.pallas_call(
        functools.partial(_norm_proj_kernel, col_chunk=512),
        out_shape=(jax.ShapeDtypeStruct((t, n), BF16), jax.ShapeDtypeStruct((t, LANES), F32)),
        grid=(t // tm,),
        in_specs=[
            pl.BlockSpec((tm, d), lambda i: (i, 0)),
            pl.BlockSpec((1, d), lambda i: (0, 0)),
            pl.BlockSpec((d, n), lambda i: (0, 0)),
            pl.BlockSpec((d, LANES), lambda i: (0, 0)),
        ],
        out_specs=(pl.BlockSpec((tm, n), lambda i: (i, 0)), pl.BlockSpec((tm, LANES), lambda i: (i, 0))),
        compiler_params=_params(("parallel",)),
        name="norm_proj",
    )(x, g, w_main, w_gate)


def _even_mixer_kernel(zq_ref, zk_ref, zv_ref, zo_ref, zu_ref, zs_ref, gt_ref, gb_ref, cw_ref, cb_ref,
                       mlg_ref, sgg_ref, sgw_ref, sgb_ref, y_ref,
                       c_sc, n_sc, m_sc, tail_sc):
    L = CHUNK

    @pl.when(pl.program_id(1) == 0)
    def _():
        c_sc[...] = jnp.zeros_like(c_sc)
        n_sc[...] = jnp.zeros_like(n_sc)
        m_sc[...] = jnp.zeros_like(m_sc)
        tail_sc[...] = jnp.zeros_like(tail_sc)

    qk_raw = jnp.concatenate([zq_ref[...], zk_ref[...]], axis=1).astype(F32)
    ext = jnp.concatenate([tail_sc[...], qk_raw], axis=0)
    conv = cb_ref[...] + cw_ref[CONV_TAPS - 1:CONV_TAPS, :] * qk_raw
    for j in range(CONV_TAPS - 1):
        lo = 8 - (CONV_TAPS - 1) + j
        conv = conv + cw_ref[j:j + 1, :] * ext[lo:lo + L, :]
    tail_sc[...] = qk_raw[L - 8:, :]
    qk = conv * _sigmoid(conv)
    q_all = qk[:, :MIX_WIDTH] * (HEAD_DIM ** -0.5)
    k_all = qk[:, MIX_WIDTH:]

    gates = gt_ref[...] + gb_ref[...]
    row = lax.broadcasted_iota(jnp.int32, (8, L), 0)
    lgate = jnp.where(row < N_HEADS, gates, _log_sigmoid(gates))
    src = lax.broadcasted_iota(jnp.int32, (L, L), 0)
    dst = lax.broadcasted_iota(jnp.int32, (L, L), 1)
    upper = (src <= dst).astype(F32)
    csum = jnp.dot(lgate, upper, precision=HIGHEST, preferred_element_type=F32)
    rows8 = jnp.where(row < N_HEADS, gates, csum)
    cols = jnp.concatenate([rows8, jnp.zeros((L - 8, L), F32)], axis=0).T
    causal = dst <= src

    for h in range(N_HEADS):
        hs = slice(h * HEAD_DIM, (h + 1) * HEAD_DIM)
        li_row = rows8[h:h + 1, :]
        b_row = rows8[N_HEADS + h:N_HEADS + h + 1, :]
        li_col = cols[:, h:h + 1]
        b_col = cols[:, N_HEADS + h:N_HEADS + h + 1]
        m_prev = m_sc[h][0:1, 0:1]
        c_prev = c_sc[h]
        n_prev = n_sc[h][0:1, :]

        qh = q_all[:, hs]
        kh = k_all[:, hs]
        vh = zv_ref[:, hs]
        qh16 = qh.astype(BF16)
        kh16 = kh.astype(BF16)

        dmat = jnp.where(causal, b_col - b_row + li_row, NEG_INF)
        inter = b_col + m_prev
        m_row = jnp.maximum(inter, jnp.max(dmat, axis=-1, keepdims=True))
        w_intra = jnp.exp(dmat - m_row)
        w_inter = jnp.exp(inter - m_row)
        s = lax.dot_general(qh16, kh16, (((1,), (1,)), ((), ())), preferred_element_type=F32) * w_intra
        cq = lax.dot_general(qh16, c_prev.astype(BF16), (((1,), (1,)), ((), ())), preferred_element_type=F32)
        num = jnp.dot(s.astype(BF16), vh, preferred_element_type=F32) + w_inter * cq
        den = jnp.sum(s, axis=-1, keepdims=True) + w_inter * jnp.sum(qh * n_prev, axis=-1, keepdims=True)
        hm = num / jnp.maximum(jnp.abs(den), jnp.exp(-m_row))

        b_last = b_row[:, L - 1:L]
        g_row = b_last - b_row + li_row
        m_new = jnp.maximum(b_last + m_prev, jnp.max(g_row, axis=-1, keepdims=True))
        w_col = jnp.exp(b_last - b_col + li_col - m_new)
        decay = jnp.exp(b_last + m_prev - m_new)
        vw = (vh.astype(F32) * w_col).astype(BF16)
        c_sc[h] = decay * c_prev + lax.dot_general(vw, kh16, (((0,), (0,)), ((), ())), preferred_element_type=F32)
        n_new = decay * n_prev + jnp.sum(kh * w_col, axis=0, keepdims=True)
        n_sc[h] = jnp.broadcast_to(n_new, (8, HEAD_DIM))
        m_sc[h] = jnp.broadcast_to(m_new, (8, LANES))

        hm = _rms(hm, mlg_ref[:, hs])
        y_ref[:, hs] = (_sigmoid(zo_ref[:, hs].astype(F32)) * hm).astype(y_ref.dtype)

    u = _gelu_tanh(zu_ref[...].astype(F32))
    vs = _rms(_gelu_tanh(zs_ref[...].astype(F32)), sgg_ref[...]).astype(BF16)
    for g in range(N_HEADS):
        gs = slice(g * HEAD_DIM, (g + 1) * HEAD_DIM)
        wg = jnp.where(causal, sgw_ref[g], 0.0).astype(BF16)
        mixed = jnp.dot(wg, vs[:, gs], preferred_element_type=F32) + sgb_ref[:, g:g + 1]
        y_ref[:, MIX_WIDTH + g * HEAD_DIM:MIX_WIDTH + (g + 1) * HEAD_DIM] = (u[:, gs] * mixed).astype(y_ref.dtype)


def even_mixer_core(z, gates_t, gate_bias_col, conv_w, conv_b, ml_norm_g, sg_norm_g, sg_w, sg_b_t, batch):
    t = z.shape[0]
    nc = t // batch // CHUNK
    w = MIX_WIDTH

    def zspec(col):
        return pl.BlockSpec((CHUNK, w), lambda b, c, col=col: (b * nc + c, col))

    full = lambda shape: pl.BlockSpec(shape, lambda b, c: (0,) * len(shape))
    return pl.pallas_call(
        _even_mixer_kernel,
        out_shape=jax.ShapeDtypeStruct((t, 2 * w), BF16),
        grid=(batch, nc),
        in_specs=[zspec(0), zspec(1), zspec(2), zspec(3), zspec(4), zspec(5),
                  pl.BlockSpec((8, CHUNK), lambda b, c: (0, b * nc + c)),
                  full((8, 1)), full((CONV_TAPS, 2 * w)), full((1, 2 * w)),
                  full((1, w)), full((1, w)), full((N_HEADS, CHUNK, CHUNK)), full((CHUNK, N_HEADS))],
        out_specs=pl.BlockSpec((CHUNK, 2 * w), lambda b, c: (b * nc + c, 0)),
        scratch_shapes=[pltpu.VMEM((N_HEADS, HEAD_DIM, HEAD_DIM), F32),
                        pltpu.VMEM((N_HEADS, 8, HEAD_DIM), F32),
                        pltpu.VMEM((N_HEADS, 8, LANES), F32),
                        pltpu.VMEM((8, 2 * w), F32)],
        compiler_params=_params(("parallel", "arbitrary")),
        name="even_mixer",
    )(z, z, z, z, z, z, gates_t, gate_bias_col, conv_w, conv_b, ml_norm_g, sg_norm_g, sg_w, sg_b_t)


def _proj_residual_kernel(*refs, n_in):
    res_ref = refs[2 * n_in]
    out_ref = refs[2 * n_in + 1]
    acc = res_ref[...]
    for a_ref, w_ref in zip(refs[:n_in], refs[n_in:2 * n_in]):
        acc = acc + jnp.dot(a_ref[...], w_ref[...], preferred_element_type=F32)
    out_ref[...] = acc


def proj_residual(acts, weights, res):
    t, d = res.shape
    tm = min(TM_PROJ, t)
    n_in = len(acts)
    in_specs = [pl.BlockSpec((tm, a.shape[1]), lambda i: (i, 0)) for a in acts]
    in_specs += [pl.BlockSpec(w.shape, lambda i: (0, 0)) for w in weights]
    in_specs += [pl.BlockSpec((tm, d), lambda i: (i, 0))]
    return pl.pallas_call(
        functools.partial(_proj_residual_kernel, n_in=n_in),
        out_shape=jax.ShapeDtypeStruct((t, d), F32),
        grid=(t // tm,),
        in_specs=in_specs,
        out_specs=pl.BlockSpec((tm, d), lambda i: (i, 0)),
        compiler_params=_params(("parallel",)),
        name="proj_residual",
    )(*acts, *weights, res)


def _swiglu_accumulate(hn, wgu_ref, wd_ref, d_ff, chunk, after_chunk=None):
    acc = None
    n_chunks = d_ff // chunk
    for c in range(n_chunks):
        g = jnp.dot(hn, wgu_ref[:, c * chunk:(c + 1) * chunk], preferred_element_type=F32)
        u = jnp.dot(hn, wgu_ref[:, d_ff + c * chunk:d_ff + (c + 1) * chunk], preferred_element_type=F32)
        a = (g * _sigmoid(g) * u).astype(BF16)
        part = jnp.dot(a, wd_ref[c * chunk:(c + 1) * chunk, :], preferred_element_type=F32)
        acc = part if acc is None else acc + part
        if after_chunk is not None:
            after_chunk(c, n_chunks)
    return acc


def _dense_ffn_kernel(x_ref, g_ref, wgu_ref, wd_ref, out_ref, *, d_ff, chunk):
    x = x_ref[...]
    hn = _rms(x, g_ref[...]).astype(BF16)
    out_ref[...] = x + _swiglu_accumulate(hn, wgu_ref, wd_ref, d_ff, chunk)


def dense_ffn(x, g, w_gate_up, w_down):
    t, d = x.shape
    d_ff = w_down.shape[0]
    tm = min(TM_PROJ, t)
    chunk = FF_CHUNK_DENSE if d_ff % FF_CHUNK_DENSE == 0 else d_ff
    return pl.pallas_call(
        functools.partial(_dense_ffn_kernel, d_ff=d_ff, chunk=chunk),
        out_shape=jax.ShapeDtypeStruct((t, d), F32),
        grid=(t // tm,),
        in_specs=[pl.BlockSpec((tm, d), lambda i: (i, 0)),
                  pl.BlockSpec((1, d), lambda i: (0, 0)),
                  pl.BlockSpec(w_gate_up.shape, lambda i: (0, 0)),
                  pl.BlockSpec(w_down.shape, lambda i: (0, 0))],
        out_specs=pl.BlockSpec((tm, d), lambda i: (i, 0)),
        compiler_params=_params(("parallel",), VMEM_LIMIT_MOE),
        name="dense_ffn",
    )(x, g, w_gate_up, w_down)


def _odd_prep_kernel(dq_ref, dk_ref, gz_ref, fb_ref, ca_ref, sm_ref, sp_ref,
                     q1_ref, q2_ref, kr_ref, f_ref, carry_sc):
    tm = dq_ref.shape[0]

    @pl.when(pl.program_id(1) == 0)
    def _():
        carry_sc[...] = jnp.zeros_like(carry_sc)

    cos_a = ca_ref[...]
    sin_m = sm_ref[...]
    sin_p = sp_ref[...]
    lane = lax.broadcasted_iota(jnp.int32, (tm, LANES), 1)
    first_map = lane < DIFF_QK
    for h in range(N_HEADS):
        hs = slice(h * LANES, (h + 1) * LANES)
        for src_ref, is_q in ((dq_ref, True), (dk_ref, False)):
            x = src_ref[:, hs].astype(F32)
            rot = (x * cos_a + pltpu.roll(x, LANES - ROPE_DIMS // 2, axis=1) * sin_m
                   + pltpu.roll(x, ROPE_DIMS // 2, axis=1) * sin_p)
            if is_q:
                q1_ref[:, hs] = jnp.where(first_map, rot, 0.0).astype(q1_ref.dtype)
                q2_ref[:, hs] = jnp.where(first_map, 0.0, rot).astype(q2_ref.dtype)
            else:
                kr_ref[:, hs] = rot.astype(kr_ref.dtype)

    lf = _log_sigmoid(gz_ref[...] + fb_ref[...])
    r = lax.broadcasted_iota(jnp.int32, (tm, tm), 0)
    c = lax.broadcasted_iota(jnp.int32, (tm, tm), 1)
    lower = (c <= r).astype(F32)
    fcum = jnp.dot(lower, lf, precision=HIGHEST, preferred_element_type=F32) + carry_sc[0:1, :]
    f_ref[...] = fcum * LOG2E
    carry_sc[...] = jnp.broadcast_to(fcum[tm - 1:tm, :], carry_sc.shape)


def odd_prep(z, gz, f_bias_row, cos_a, sin_m, sin_p, batch):
    t = z.shape[0]
    seq = t // batch
    tm = min(TM_PROJ, seq)
    ns = seq // tm
    w = MIX_WIDTH
    tok = lambda col: pl.BlockSpec((tm, w), lambda b, s, col=col: (b * ns + s, col))
    tab = pl.BlockSpec((tm, LANES), lambda b, s: (s, 0))
    out_tok = pl.BlockSpec((tm, w), lambda b, s: (b * ns + s, 0))
    return pl.pallas_call(
        _odd_prep_kernel,
        out_shape=(jax.ShapeDtypeStruct((t, w), BF16), jax.ShapeDtypeStruct((t, w), BF16),
                   jax.ShapeDtypeStruct((t, w), BF16), jax.ShapeDtypeStruct((t, LANES), F32)),
        grid=(batch, ns),
        in_specs=[tok(3), tok(4),
                  pl.BlockSpec((tm, LANES), lambda b, s: (b * ns + s, 0)),
                  pl.BlockSpec((1, LANES), lambda b, s: (0, 0)),
                  tab, tab, tab],
        out_specs=(out_tok, out_tok, out_tok, pl.BlockSpec((tm, LANES), lambda b, s: (b * ns + s, 0))),
        scratch_shapes=[pltpu.VMEM((8, LANES), F32)],
        compiler_params=_params(("parallel", "arbitrary")),
        name="odd_prep",
    )(z, z, gz, f_bias_row, cos_a, sin_m, sin_p)


def _softmax_chunk(q, k, v, fk, fq, m_ref, l_ref, acc_ref, mask):
    s = lax.dot_general(q, k, (((1,), (1,)), ((), ())), preferred_element_type=F32)
    if fk is not None:
        s = s - fk
    if mask is not None:
        s = jnp.where(mask, s, NEG_INF)
    row_max = jnp.max(s, axis=-1, keepdims=True)
    if fq is not None:
        row_max = row_max + fq
    m_prev = m_ref[...]
    m_new = jnp.maximum(m_prev, row_max)
    p = jnp.exp2(s - (m_new if fq is None else m_new - fq))
    alpha = jnp.exp2(m_prev - m_new)
    l_ref[...] = alpha * l_ref[...] + jnp.sum(p, axis=-1, keepdims=True)
    acc_ref[...] = alpha * acc_ref[...] + jnp.dot(p.astype(BF16), v, preferred_element_type=F32)
    m_ref[...] = m_new


def _causal_sweep(i, tq, ch, streams):
    for st in streams:
        st["m"][...] = jnp.full_like(st["m"], -1e30)
        st["l"][...] = jnp.zeros_like(st["l"])
        st["acc"][...] = jnp.zeros_like(st["acc"])

    def chunk(j, mask):
        start = pl.multiple_of(j * ch, ch)
        for st in streams:
            fk = st["fk"](j) if st["fk"] is not None else None
            _softmax_chunk(st["q"], st["k"](start), st["v"](start), fk, st["fq"], st["m"], st["l"], st["acc"], mask)

    def body(j, carry):
        chunk(j, None)
        return carry

    n_full = (i * tq) // ch
    lax.fori_loop(0, n_full, body, 0)
    r = lax.broadcasted_iota(jnp.int32, (tq, ch), 0)
    c = lax.broadcasted_iota(jnp.int32, (tq, ch), 1)
    chunk(n_full, (c - r) <= (i * tq - n_full * ch))


def _fox_attention_kernel(q_ref, k_ref, v_ref, fcol_ref, frow_ref, out_ref, *scratch, tq, ch):
    hp = pl.program_id(1)
    i = pl.program_id(2)
    lane = lax.broadcasted_iota(jnp.int32, (tq, LANES), 1)
    fcol = fcol_ref[...]
    streams = []
    for n in range(2):
        hs = slice(n * LANES, (n + 1) * LANES)
        streams.append(dict(
            q=q_ref[:, hs],
            k=lambda start, hs=hs: k_ref[pl.ds(start, ch), hs],
            v=lambda start, hs=hs: v_ref[pl.ds(start, ch), hs],
            fk=lambda j, n=n: frow_ref[n, j],
            fq=jnp.sum(jnp.where(lane == 2 * hp + n, fcol, 0.0), axis=-1, keepdims=True),
            m=scratch[3 * n], l=scratch[3 * n + 1], acc=scratch[3 * n + 2]))
    _causal_sweep(i, tq, ch, streams)
    for n, st in enumerate(streams):
        out_ref[:, n * LANES:(n + 1) * LANES] = (st["acc"][...] / st["l"][...]).astype(out_ref.dtype)


def _diff_attention_kernel(q1_ref, q2_ref, k_ref, v_ref, lam_ref, ng_ref, out_ref, *scratch, tq, ch, lambda_init):
    i = pl.program_id(2)
    streams = []
    for n, q_ref in enumerate((q1_ref, q2_ref)):
        streams.append(dict(
            q=q_ref[...],
            k=lambda start: k_ref[pl.ds(start, ch), :],
            v=lambda start: v_ref[pl.ds(start, ch), :],
            fk=None, fq=None,
            m=scratch[3 * n], l=scratch[3 * n + 1], acc=scratch[3 * n + 2]))
    _causal_sweep(i, tq, ch, streams)
    o1, o2 = (st["acc"][...] / st["l"][...] for st in streams)
    lam_p = lam_ref[...]
    lam = (jnp.exp(jnp.sum(lam_p[0:1, :] * lam_p[1:2, :], axis=-1, keepdims=True))
           - jnp.exp(jnp.sum(lam_p[2:3, :] * lam_p[3:4, :], axis=-1, keepdims=True)) + lambda_init)
    out_ref[...] = (_rms(o1 - lam * o2, ng_ref[...]) * (1.0 - lambda_init)).astype(out_ref.dtype)


def _attention_tiles(seq):
    tq = min(ATT_TQ, seq)
    ch = min(ATT_CHUNK, seq)
    assert seq % ch == 0 and ch % tq == 0
    return tq, ch


def _stream_scratch(tq):
    return [pltpu.VMEM((tq, 1), F32), pltpu.VMEM((tq, 1), F32), pltpu.VMEM((tq, LANES), F32)] * 2


def fox_attention(z, fcol, frow, batch):
    t = z.shape[0]
    seq = t // batch
    tq, ch = _attention_tiles(seq)
    nq = seq // tq
    pairs = N_HEADS // 2
    w2 = 2 * LANES
    return pl.pallas_call(
        functools.partial(_fox_attention_kernel, tq=tq, ch=ch),
        out_shape=jax.ShapeDtypeStruct((t, MIX_WIDTH), BF16),
        grid=(batch, pairs, nq),
        in_specs=[pl.BlockSpec((tq, w2), lambda b, hp, i: (b * nq + i, hp)),
                  pl.BlockSpec((seq, w2), lambda b, hp, i: (b, pairs + hp)),
                  pl.BlockSpec((seq, w2), lambda b, hp, i: (b, 2 * pairs + hp)),
                  pl.BlockSpec((tq, LANES), lambda b, hp, i: (b * nq + i, 0)),
                  pl.BlockSpec((2, seq // ch, 1, ch), lambda b, hp, i: (b * pairs + hp, 0, 0, 0))],
        out_specs=pl.BlockSpec((tq, w2), lambda b, hp, i: (b * nq + i, hp)),
        scratch_shapes=_stream_scratch(tq),
        compiler_params=_params(("parallel", "parallel", "arbitrary")),
        name="fox_attention",
    )(z, z, z, fcol, frow)


def diff_attention(q1, q2, kr, z, v_col0, lam_p, ng, lambda_init, batch):
    t = z.shape[0]
    seq = t // batch
    tq, ch = _attention_tiles(seq)
    nq = seq // tq
    qspec = pl.BlockSpec((tq, LANES), lambda b, h, i: (b * nq + i, h))
    return pl.pallas_call(
        functools.partial(_diff_attention_kernel, tq=tq, ch=ch, lambda_init=lambda_init),
        out_shape=jax.ShapeDtypeStruct((t, MIX_WIDTH), BF16),
        grid=(batch, N_HEADS, nq),
        in_specs=[qspec, qspec,
                  pl.BlockSpec((seq, LANES), lambda b, h, i: (b, h)),
                  pl.BlockSpec((seq, LANES), lambda b, h, i: (b, v_col0 + h)),
                  pl.BlockSpec(lam_p.shape, lambda b, h, i: (0, 0)),
                  pl.BlockSpec(ng.shape, lambda b, h, i: (0, 0))],
        out_specs=qspec,
        scratch_shapes=_stream_scratch(tq),
        compiler_params=_params(("parallel", "parallel", "arbitrary")),
        name="diff_attention",
    )(q1, q2, kr, z, lam_p, ng)


def _router_kernel(x_ref, g_ref, wr_ref, hp_ref, meta_ref, cnt_ref, carry_sc):
    tm = x_ref.shape[0]

    @pl.when(pl.program_id(0) == 0)
    def _():
        carry_sc[...] = jnp.zeros_like(carry_sc)

    hn = _rms(x_ref[...], g_ref[...])
    hp_ref[...] = hn
    lane = lax.broadcasted_iota(jnp.int32, (tm, LANES), 1)
    logits = jnp.dot(hn, wr_ref[...], precision=HIGHEST, preferred_element_type=F32)
    logits = jnp.where(lane < N_EXPERTS, logits, NEG_INF)
    m1 = jnp.max(logits, axis=-1, keepdims=True)
    i1 = jnp.min(jnp.where(logits == m1, lane, LANES), axis=-1, keepdims=True)
    rest = jnp.where(lane == i1, NEG_INF, logits)
    m2 = jnp.max(rest, axis=-1, keepdims=True)
    i2 = jnp.min(jnp.where(rest == m2, lane, LANES), axis=-1, keepdims=True)
    e2 = jnp.exp(m2 - m1)
    w1 = 1.0 / (1.0 + e2)
    w2 = e2 * w1

    sel1 = lane == i1
    sel2 = lane == i2
    onehot = jnp.where(sel1, 1.0, 0.0) + jnp.where(sel2, 1.0, 0.0)
    r = lax.broadcasted_iota(jnp.int32, (tm, tm), 0)
    c = lax.broadcasted_iota(jnp.int32, (tm, tm), 1)
    lower = (c <= r).astype(BF16)
    incl = jnp.dot(lower, onehot.astype(BF16), preferred_element_type=F32)
    before = incl - onehot + carry_sc[0:1, :]
    r1 = jnp.sum(jnp.where(sel1, before, 0.0), axis=-1, keepdims=True)
    r2 = jnp.sum(jnp.where(sel2, before, 0.0), axis=-1, keepdims=True)
    total = incl[tm - 1:tm, :] + carry_sc[0:1, :]
    carry_sc[...] = jnp.broadcast_to(total, carry_sc.shape)
    cnt_ref[...] = jnp.broadcast_to(total, cnt_ref.shape)

    meta = jnp.where(lane == 0, i1.astype(F32), 0.0)
    meta = jnp.where(lane == 1, i2.astype(F32), meta)
    meta = jnp.where(lane == 2, r1, meta)
    meta = jnp.where(lane == 3, r2, meta)
    meta = jnp.where(lane == 4, w1, meta)
    meta = jnp.where(lane == 5, w2, meta)
    meta_ref[...] = meta


def moe_router(x, g, w_router_pad):
    t, d = x.shape
    tm = min(TM_PROJ, t)
    return pl.pallas_call(
        _router_kernel,
        out_shape=(jax.ShapeDtypeStruct((t, d), F32), jax.ShapeDtypeStruct((t, LANES), F32),
                   jax.ShapeDtypeStruct((8, LANES), F32)),
        grid=(t // tm,),
        in_specs=[pl.BlockSpec((tm, d), lambda i: (i, 0)),
                  pl.BlockSpec((1, d), lambda i: (0, 0)),
                  pl.BlockSpec((d, LANES), lambda i: (0, 0))],
        out_specs=(pl.BlockSpec((tm, d), lambda i: (i, 0)), pl.BlockSpec((tm, LANES), lambda i: (i, 0)),
                   pl.BlockSpec((8, LANES), lambda i: (0, 0))),
        scratch_shapes=[pltpu.VMEM((8, LANES), F32)],
        compiler_params=_params(("arbitrary",)),
        name="moe_router",
    )(x, g, w_router_pad)


def _expert_ffn_kernel(te_ref, nt_ref, tok_ref, dst_ref, hn_ref, wgu_ref, wd_ref, planes_ref,
                       xbuf, ybuf, gsem, ssem, zsem, *, d_ff, chunk, tm, n_pad_tiles):
    del te_ref
    i = pl.program_id(0)
    nt = nt_ref[0]
    slot = i % 2
    other = 1 - slot
    n_real = planes_ref.shape[0] - n_pad_tiles * tm

    def gather_row(tile, s, r):
        return pltpu.make_async_copy(hn_ref.at[pl.ds(tok_ref[tile * tm + r], 1)], xbuf.at[s, pl.ds(r, 1)], gsem.at[s])

    def scatter_row(tile, s, r):
        return pltpu.make_async_copy(ybuf.at[s, pl.ds(r, 1)], planes_ref.at[pl.ds(dst_ref[tile * tm + r], 1)],
                                     ssem.at[s])

    def start_rows(make_copy, tile, s):
        def body(r, carry):
            make_copy(tile, s, r).start()
            return carry

        lax.fori_loop(0, tm, body, 0, unroll=8)

    def wait_gather(s):
        pltpu.make_async_copy(hn_ref.at[pl.ds(0, tm)], xbuf.at[s], gsem.at[s]).wait()

    def wait_scatter(s):
        pltpu.make_async_copy(ybuf.at[s], planes_ref.at[pl.ds(0, tm)], ssem.at[s]).wait()

    def zero_fill(p):
        return pltpu.make_async_copy(ybuf.at[1], planes_ref.at[pl.ds(n_real + p * tm, tm)], zsem)

    @pl.when(i == 0)
    def _():
        start_rows(gather_row, 0, 0)
        ybuf[1] = jnp.zeros((tm, ybuf.shape[2]), F32)
        for p in range(n_pad_tiles):
            zero_fill(p).start()
        for p in range(n_pad_tiles):
            zero_fill(p).wait()

    @pl.when(i < nt)
    def _():
        wait_gather(slot)

        @pl.when(i >= 2)
        def _():
            wait_scatter(slot)

        def issue_rows(c, n_chunks):
            for r in range(c * tm // n_chunks, (c + 1) * tm // n_chunks):
                @pl.when(i + 1 < nt)
                def _():
                    gather_row(i + 1, other, r).start()

                @pl.when(i >= 1)
                def _():
                    scatter_row(i - 1, other, r).start()

        ybuf[slot] = _swiglu_accumulate(xbuf[slot].astype(BF16), wgu_ref.at[0], wd_ref.at[0], d_ff, chunk,
                                        after_chunk=issue_rows)

    @pl.when(i == nt)
    def _():
        @pl.when(i >= 2)
        def _():
            wait_scatter(slot)

        start_rows(scatter_row, i - 1, other)
        wait_scatter(other)


def moe_expert_ffn(tile_expert, n_tiles, row_token, row_dst, hn, w_gate_up, w_down):
    t, d = hn.shape
    n_exp, d_ff = w_down.shape[:2]
    tm = min(TM_MOE, 2 * t)
    n_grid = row_token.shape[0] // tm
    chunk = FF_CHUNK_MOE if d_ff % FF_CHUNK_MOE == 0 else d_ff
    w_index = lambda i, te, nt, tok, dst: (te[i], 0, 0)
    return pl.pallas_call(
        functools.partial(_expert_ffn_kernel, d_ff=d_ff, chunk=chunk, tm=tm, n_pad_tiles=n_exp),
        out_shape=jax.ShapeDtypeStruct((2 * t + n_exp * tm, d), F32),
        grid_spec=pltpu.PrefetchScalarGridSpec(
            num_scalar_prefetch=4,
            grid=(n_grid,),
            in_specs=[pl.BlockSpec(memory_space=pl.ANY),
                      pl.BlockSpec((1, d, 2 * d_ff), w_index),
                      pl.BlockSpec((1, d_ff, d), w_index)],
            out_specs=pl.BlockSpec(memory_space=pl.ANY),
            scratch_shapes=[pltpu.VMEM((2, tm, d), F32), pltpu.VMEM((2, tm, d), F32),
                            pltpu.SemaphoreType.DMA((2,)), pltpu.SemaphoreType.DMA((2,)),
                            pltpu.SemaphoreType.DMA(())]),
        compiler_params=_params(("arbitrary",), VMEM_LIMIT_MOE),
        name="moe_expert_ffn",
    )(tile_expert, n_tiles, row_token, row_dst, hn, w_gate_up, w_down)


def _combine_kernel(x_ref, meta_ref, g_ref, p0_ref, p1_ref, out_ref):
    meta = meta_ref[...]
    y = x_ref[...] + meta[:, 4:5] * p0_ref[...] + meta[:, 5:6] * p1_ref[...]
    out_ref[...] = _rms(y, g_ref[...])


def moe_combine(x, meta, g_final, planes):
    t, d = x.shape
    tm = min(TM_PROJ, t)
    nb = t // tm
    return pl.pallas_call(
        _combine_kernel,
        out_shape=jax.ShapeDtypeStruct((t, d), F32),
        grid=(nb,),
        in_specs=[pl.BlockSpec((tm, d), lambda i: (i, 0)),
                  pl.BlockSpec((tm, LANES), lambda i: (i, 0)),
                  pl.BlockSpec((1, d), lambda i: (0, 0)),
                  pl.BlockSpec((tm, d), lambda i: (i, 0)),
                  pl.BlockSpec((tm, d), lambda i: (nb + i, 0))],
        out_specs=pl.BlockSpec((tm, d), lambda i: (i, 0)),
        compiler_params=_params(("parallel",)),
        name="moe_combine",
    )(x, meta, g_final, planes, planes)


def moe_layer(x, g_ffn, w_router, w_gate_up, w_down, g_final):
    t, d = x.shape
    n_exp = w_router.shape[1]
    tm = min(TM_MOE, 2 * t)
    n_rows = 2 * t + n_exp * tm
    wr_pad = jnp.zeros((d, LANES), F32).at[:, :n_exp].set(w_router)
    hn, meta, counts = moe_router(x, g_ffn, wr_pad)

    cnt = counts[0, :n_exp].astype(jnp.int32)
    padded = ((cnt + tm - 1) // tm) * tm
    ends = jnp.cumsum(padded)
    starts = ends - padded
    e_idx = meta[:, 0:2].astype(jnp.int32)
    rank = meta[:, 2:4].astype(jnp.int32)
    start_of = jnp.sum(jnp.where(e_idx[..., None] == jnp.arange(n_exp), starts, 0), axis=-1)
    pos = (start_of + rank).T.reshape(-1)
    tile_start = jnp.arange(n_rows // tm, dtype=jnp.int32) * tm
    tile_expert = jnp.minimum(jnp.sum(tile_start[:, None] >= ends[None, :], axis=-1), n_exp - 1).astype(jnp.int32)
    n_tiles = (ends[-1:] // tm).astype(jnp.int32)
    pad_dst = (2 * t + tile_expert[:, None] * tm + jnp.arange(tm, dtype=jnp.int32)[None, :]).reshape(-1)
    served = jnp.full((n_rows,), -1, jnp.int32).at[pos].set(jnp.arange(2 * t, dtype=jnp.int32), unique_indices=True)
    row_dst = jnp.where(served >= 0, served, pad_dst)
    row_token = jnp.where(served >= 0, served % t, 0)

    planes = moe_expert_ffn(tile_expert, n_tiles, row_token, row_dst, hn, w_gate_up, w_down)
    return moe_combine(x, meta, g_final, planes)


def _rope_tables(seq):
    half = ROPE_DIMS // 2
    inv = ROPE_THETA ** (-jnp.arange(half, dtype=F32) / half)
    ang = jnp.arange(seq, dtype=F32)[:, None] * inv[None, :]
    cos, sin = jnp.cos(ang), jnp.sin(ang)
    ones = jnp.ones((seq, DIFF_QK - ROPE_DIMS), F32)
    zeros = jnp.zeros((seq, DIFF_QK - ROPE_DIMS), F32)
    zh = jnp.zeros((seq, half), F32)
    cos_a = jnp.concatenate([cos, cos, ones], axis=1)
    sin_m = jnp.concatenate([-sin, zh, zeros], axis=1)
    sin_p = jnp.concatenate([zh, sin, zeros], axis=1)
    tile2 = lambda a: jnp.concatenate([a, a], axis=1)
    return tile2(cos_a), tile2(sin_m), tile2(sin_p)


def even_layer(x, batch, norm_mix, w_in, conv_w, conv_b, gate_b, ml_norm_g, sg_norm_g, sg_w, sg_b, w_out,
               norm_ffn, w_gate_up, w_down):
    w = MIX_WIDTH
    n_gate = 2 * N_HEADS
    w_main = jnp.concatenate([w_in[:, :4 * w], w_in[:, 4 * w + n_gate:]], axis=1).astype(BF16)
    w_gate = jnp.zeros((w_in.shape[0], LANES), F32).at[:, :n_gate].set(w_in[:, 4 * w:4 * w + n_gate]).astype(BF16)
    z, gz = norm_proj(x, norm_mix[None, :], w_main, w_gate)
    gates_t = gz[:, :n_gate].T
    y = even_mixer_core(z, gates_t, gate_b.reshape(n_gate, 1), conv_w, conv_b[None, :], ml_norm_g[None, :],
                        sg_norm_g[None, :], sg_w, sg_b.T, batch)
    x = proj_residual([y], [w_out.astype(BF16)], x)
    return dense_ffn(x, norm_ffn[None, :], w_gate_up.astype(BF16), w_down.astype(BF16))


def odd_mixer_layer(x, batch, norm_mix, w_in, fox_f_b, diff_lambda, diff_norm_g, w_out, lambda_init):
    w = MIX_WIDTH
    t = x.shape[0]
    seq = t // batch
    w_main = jnp.concatenate([w_in[:, :w] * (LOG2E * HEAD_DIM ** -0.5), w_in[:, w:3 * w],
                              w_in[:, 3 * w + N_HEADS:4 * w + N_HEADS] * (LOG2E * DIFF_QK ** -0.5),
                              w_in[:, 4 * w + N_HEADS:]], axis=1).astype(BF16)
    w_gate = jnp.zeros((w_in.shape[0], LANES), F32).at[:, :N_HEADS].set(w_in[:, 3 * w:3 * w + N_HEADS]).astype(BF16)
    z, gz = norm_proj(x, norm_mix[None, :], w_main, w_gate)
    f_bias_row = jnp.zeros((1, LANES), F32).at[0, :N_HEADS].set(fox_f_b)
    cos_a, sin_m, sin_p = _rope_tables(seq)
    q1, q2, kr, fcol = odd_prep(z, gz, f_bias_row, cos_a, sin_m, sin_p, batch)
    _, ch = _attention_tiles(seq)
    frow = fcol[:, :N_HEADS].reshape(batch, seq, N_HEADS).transpose(0, 2, 1).reshape(
        batch * N_HEADS, seq // ch, 1, ch)
    y_fox = fox_attention(z, fcol, frow, batch)
    y_diff = diff_attention(q1, q2, kr, z, 5 * N_HEADS, diff_lambda, diff_norm_g[None, :], lambda_init, batch)
    w_out16 = w_out.astype(BF16)
    return proj_residual([y_fox, y_diff], [w_out16[:w], w_out16[w:]], x)


def kernel(x, even_norm_mix, even_w_in, even_ml_conv_w, even_ml_conv_b, even_ml_gate_b, even_ml_norm_g,
           even_sg_norm_g, even_sg_w, even_sg_b, even_w_out, even_norm_ffn, ffn_w_gate_up, ffn_w_down,
           odd_norm_mix, odd_w_in, odd_fox_f_b, odd_diff_lambda, odd_diff_norm_g, odd_w_out, odd_norm_ffn,
           moe_w_router, moe_w_gate_up, moe_w_down, final_norm):
    batch, seq, d = x.shape
    h = x.reshape(batch * seq, d)
    h = even_layer(h, batch, even_norm_mix[0], even_w_in[0], even_ml_conv_w[0], even_ml_conv_b[0],
                   even_ml_gate_b[0], even_ml_norm_g[0], even_sg_norm_g[0], even_sg_w[0], even_sg_b[0],
                   even_w_out[0], even_norm_ffn[0], ffn_w_gate_up[0], ffn_w_down[0])
    lambda_init = 0.8 - 0.6 * math.exp(-0.3 * 1)
    h = odd_mixer_layer(h, batch, odd_norm_mix[0], odd_w_in[0], odd_fox_f_b[0], odd_diff_lambda[0],
                        odd_diff_norm_g[0], odd_w_out[0], lambda_init)
    out = moe_layer(h, odd_norm_ffn[0][None, :], moe_w_router[0], moe_w_gate_up[0].astype(BF16),
                    moe_w_down[0].astype(BF16), final_norm[None, :])
    return out.reshape(batch, seq, d)
```

```python
import functools
import math

import jax
import jax.numpy as jnp
from jax import lax
from jax.experimental import pallas as pl
from jax.experimental.pallas import tpu as pltpu

F32 = jnp.float32
BF16 = jnp.bfloat16
HIGHEST = lax.Precision.HIGHEST

EPS = 1e-6
CHUNK = 128
HEAD_DIM = 128
N_HEADS = 4
MIX_WIDTH = N_HEADS * HEAD_DIM
CONV_TAPS = 4
DIFF_QK = 64
ROPE_DIMS = DIFF_QK // 4
ROPE_THETA = 500000.0
N_EXPERTS = 8
LANES = 128
NEG_INF = float("-inf")

VMEM_LIMIT_DEFAULT = 48 * 1024 * 1024
VMEM_LIMIT_MOE = 60 * 1024 * 1024

TM_PROJ = 512
TM_MOE = 512
FF_CHUNK_DENSE = 256
FF_CHUNK_MOE = 512
ATT_TQ = 512
ATT_CHUNK = 2048
LOG2E = math.log2(math.e)


def _params(semantics, vmem=VMEM_LIMIT_DEFAULT):
    return pltpu.CompilerParams(dimension_semantics=semantics, vmem_limit_bytes=vmem)


def _rms(x, g):
    return x * lax.rsqrt(jnp.mean(x * x, axis=-1, keepdims=True) + EPS) * g


def _sigmoid(x):
    return 1.0 / (1.0 + jnp.exp(-x))


def _log_sigmoid(x):
    return jnp.minimum(x, 0.0) - jnp.log1p(jnp.exp(-jnp.abs(x)))


def _gelu_tanh(x):
    return 0.5 * x * (1.0 + jnp.tanh(math.sqrt(2.0 / math.pi) * (x + 0.044715 * (x * x * x))))


def _norm_proj_kernel(x_ref, g_ref, w_ref, wg_ref, z_ref, gz_ref, *, col_chunk):
    hn = _rms(x_ref[...], g_ref[...]).astype(BF16)
    for c in range(w_ref.shape[1] // col_chunk):
        cols = slice(c * col_chunk, (c + 1) * col_chunk)
        z_ref[:, cols] = jnp.dot(hn, w_ref[:, cols], preferred_element_type=F32).astype(z_ref.dtype)
    gz_ref[...] = jnp.dot(hn, wg_ref[...], preferred_element_type=F32)


def norm_proj(x, g, w_main, w_gate):
    t, d = x.shape
    n = w_main.shape[1]
    tm = min(TM_PROJ, t)
    return pl.pallas_call(
        functools.partial(_norm_proj_kernel, col_chunk=512),
        out_shape=(jax.ShapeDtypeStruct((t, n), BF16), jax.ShapeDtypeStruct((t, LANES), F32)),
        grid=(t // tm,),
        in_specs=[
            pl.BlockSpec((tm, d), lambda i: (i, 0)),
            pl.BlockSpec((1, d), lambda i: (0, 0)),
            pl.BlockSpec((d, n), lambda i: (0, 0)),
            pl.BlockSpec((d, LANES), lambda i: (0, 0)),
        ],
        out_specs=(pl.BlockSpec((tm, n), lambda i: (i, 0)), pl.BlockSpec((tm, LANES), lambda i: (i, 0))),
        compiler_params=_params(("parallel",)),
        name="norm_proj",
    )(x, g, w_main, w_gate)


def _even_mixer_kernel(zq_ref, zk_ref, zv_ref, zo_ref, zu_ref, zs_ref, gt_ref, gb_ref, cw_ref, cb_ref,
                       mlg_ref, sgg_ref, sgw_ref, sgb_ref, y_ref,
                       c_sc, n_sc, m_sc, tail_sc):
    L = CHUNK

    @pl.when(pl.program_id(1) == 0)
    def _():
        c_sc[...] = jnp.zeros_like(c_sc)
        n_sc[...] = jnp.zeros_like(n_sc)
        m_sc[...] = jnp.zeros_like(m_sc)
        tail_sc[...] = jnp.zeros_like(tail_sc)

    qk_raw = jnp.concatenate([zq_ref[...], zk_ref[...]], axis=1).astype(F32)
    ext = jnp.concatenate([tail_sc[...], qk_raw], axis=0)
    conv = cb_ref[...] + cw_ref[CONV_TAPS - 1:CONV_TAPS, :] * qk_raw
    for j in range(CONV_TAPS - 1):
        lo = 8 - (CONV_TAPS - 1) + j
        conv = conv + cw_ref[j:j + 1, :] * ext[lo:lo + L, :]
    tail_sc[...] = qk_raw[L - 8:, :]
    qk = conv * _sigmoid(conv)
    q_all = qk[:, :MIX_WIDTH] * (HEAD_DIM ** -0.5)
    k_all = qk[:, MIX_WIDTH:]

    gates = gt_ref[...] + gb_ref[...]
    row = lax.broadcasted_iota(jnp.int32, (8, L), 0)
    lgate = jnp.where(row < N_HEADS, gates, _log_sigmoid(gates))
    src = lax.broadcasted_iota(jnp.int32, (L, L), 0)
    dst = lax.broadcasted_iota(jnp.int32, (L, L), 1)
    upper = (src <= dst).astype(F32)
    csum = jnp.dot(lgate, upper, precision=HIGHEST, preferred_element_type=F32)
    rows8 = jnp.where(row < N_HEADS, gates, csum)
    cols = jnp.concatenate([rows8, jnp.zeros((L - 8, L), F32)], axis=0).T
    causal = dst <= src

    for h in range(N_HEADS):
        hs = slice(h * HEAD_DIM, (h + 1) * HEAD_DIM)
        li_row = rows8[h:h + 1, :]
        b_row = rows8[N_HEADS + h:N_HEADS + h + 1, :]
        li_col = cols[:, h:h + 1]
        b_col = cols[:, N_HEADS + h:N_HEADS + h + 1]
        m_prev = m_sc[h][0:1, 0:1]
        c_prev = c_sc[h]
        n_prev = n_sc[h][0:1, :]

        qh = q_all[:, hs]
        kh = k_all[:, hs]
        vh = zv_ref[:, hs]
        qh16 = qh.astype(BF16)
        kh16 = kh.astype(BF16)

        dmat = jnp.where(causal, b_col - b_row + li_row, NEG_INF)
        inter = b_col + m_prev
        m_row = jnp.maximum(inter, jnp.max(dmat, axis=-1, keepdims=True))
        w_intra = jnp.exp(dmat - m_row)
        w_inter = jnp.exp(inter - m_row)
        s = lax.dot_general(qh16, kh16, (((1,), (1,)), ((), ())), preferred_element_type=F32) * w_intra
        cq = lax.dot_general(qh16, c_prev.astype(BF16), (((1,), (1,)), ((), ())), preferred_element_type=F32)
        num = jnp.dot(s.astype(BF16), vh, preferred_element_type=F32) + w_inter * cq
        den = jnp.sum(s, axis=-1, keepdims=True) + w_inter * jnp.sum(qh * n_prev, axis=-1, keepdims=True)
        hm = num / jnp.maximum(jnp.abs(den), jnp.exp(-m_row))

        b_last = b_row[:, L - 1:L]
        g_row = b_last - b_row + li_row
        m_new = jnp.maximum(b_last + m_prev, jnp.max(g_row, axis=-1, keepdims=True))
        w_col = jnp.exp(b_last - b_col + li_col - m_new)
        decay = jnp.exp(b_last + m_prev - m_new)
        vw = (vh.astype(F32) * w_col).astype(BF16)
        c_sc[h] = decay * c_prev + lax.dot_general(vw, kh16, (((0,), (0,)), ((), ())), preferred_element_type=F32)
        n_new = decay * n_prev + jnp.sum(kh * w_col, axis=0, keepdims=True)
        n_sc[h] = jnp.broadcast_to(n_new, (8, HEAD_DIM))
        m_sc[h] = jnp.broadcast_to(m_new, (8, LANES))

        hm = _rms(hm, mlg_ref[:, hs])
        y_ref[:, hs] = (_sigmoid(zo_ref[:, hs].astype(F32)) * hm).astype(y_ref.dtype)

    u = _gelu_tanh(zu_ref[...].astype(F32))
    vs = _rms(_gelu_tanh(zs_ref[...].astype(F32)), sgg_ref[...]).astype(BF16)
    for g in range(N_HEADS):
        gs = slice(g * HEAD_DIM, (g + 1) * HEAD_DIM)
        wg = jnp.where(causal, sgw_ref[g], 0.0).astype(BF16)
        mixed = jnp.dot(wg, vs[:, gs], preferred_element_type=F32) + sgb_ref[:, g:g + 1]
        y_ref[:, MIX_WIDTH + g * HEAD_DIM:MIX_WIDTH + (g + 1) * HEAD_DIM] = (u[:, gs] * mixed).astype(y_ref.dtype)


def even_mixer_core(z, gates_t, gate_bias_col, conv_w, conv_b, ml_norm_g, sg_norm_g, sg_w, sg_b_t, batch):
    t = z.shape[0]
    nc = t // batch // CHUNK
    w = MIX_WIDTH

    def zspec(col):
        return pl.BlockSpec((CHUNK, w), lambda b, c, col=col: (b * nc + c, col))

    full = lambda shape: pl.BlockSpec(shape, lambda b, c: (0,) * len(shape))
    return pl.pallas_call(
        _even_mixer_kernel,
        out_shape=jax.ShapeDtypeStruct((t, 2 * w), BF16),
        grid=(batch, nc),
        in_specs=[zspec(0), zspec(1), zspec(2), zspec(3), zspec(4), zspec(5),
                  pl.BlockSpec((8, CHUNK), lambda b, c: (0, b * nc + c)),
                  full((8, 1)), full((CONV_TAPS, 2 * w)), full((1, 2 * w)),
                  full((1, w)), full((1, w)), full((N_HEADS, CHUNK, CHUNK)), full((CHUNK, N_HEADS))],
        out_specs=pl.BlockSpec((CHUNK, 2 * w), lambda b, c: (b * nc + c, 0)),
        scratch_shapes=[pltpu.VMEM((N_HEADS, HEAD_DIM, HEAD_DIM), F32),
                        pltpu.VMEM((N_HEADS, 8, HEAD_DIM), F32),
                        pltpu.VMEM((N_HEADS, 8, LANES), F32),
                        pltpu.VMEM((8, 2 * w), F32)],
        compiler_params=_params(("parallel", "arbitrary")),
        name="even_mixer",
    )(z, z, z, z, z, z, gates_t, gate_bias_col, conv_w, conv_b, ml_norm_g, sg_norm_g, sg_w, sg_b_t)


def _swiglu_accumulate(hn, wgu_ref, wd_ref, d_ff, chunk, after_chunk=None):
    acc = None
    n_chunks = d_ff // chunk
    for c in range(n_chunks):
        g = jnp.dot(hn, wgu_ref[:, c * chunk:(c + 1) * chunk], preferred_element_type=F32)
        u = jnp.dot(hn, wgu_ref[:, d_ff + c * chunk:d_ff + (c + 1) * chunk], preferred_element_type=F32)
        a = (g * _sigmoid(g) * u).astype(BF16)
        part = jnp.dot(a, wd_ref[c * chunk:(c + 1) * chunk, :], preferred_element_type=F32)
        acc = part if acc is None else acc + part
        if after_chunk is not None:
            after_chunk(c, n_chunks)
    return acc


def _dense_ffn_kernel(y_ref, wo_ref, res_ref, g_ref, wgu_ref, wd_ref, out_ref, *, d_ff, chunk):
    x = res_ref[...] + jnp.dot(y_ref[...], wo_ref[...], preferred_element_type=F32)
    hn = _rms(x, g_ref[...]).astype(BF16)
    out_ref[...] = x + _swiglu_accumulate(hn, wgu_ref, wd_ref, d_ff, chunk)


def proj_dense_ffn(y, w_out, res, g, w_gate_up, w_down):
    t, d = res.shape
    d_ff = w_down.shape[0]
    tm = min(TM_PROJ, t)
    chunk = FF_CHUNK_DENSE if d_ff % FF_CHUNK_DENSE == 0 else d_ff
    return pl.pallas_call(
        functools.partial(_dense_ffn_kernel, d_ff=d_ff, chunk=chunk),
        out_shape=jax.ShapeDtypeStruct((t, d), F32),
        grid=(t // tm,),
        in_specs=[pl.BlockSpec((tm, y.shape[1]), lambda i: (i, 0)),
                  pl.BlockSpec(w_out.shape, lambda i: (0, 0)),
                  pl.BlockSpec((tm, d), lambda i: (i, 0)),
                  pl.BlockSpec((1, d), lambda i: (0, 0)),
                  pl.BlockSpec(w_gate_up.shape, lambda i: (0, 0)),
                  pl.BlockSpec(w_down.shape, lambda i: (0, 0))],
        out_specs=pl.BlockSpec((tm, d), lambda i: (i, 0)),
        compiler_params=_params(("parallel",), VMEM_LIMIT_MOE),
        name="dense_ffn",
    )(y, w_out, res, g, w_gate_up, w_down)


def _odd_prep_kernel(dq_ref, dk_ref, gz_ref, fb_ref, ca_ref, sm_ref, sp_ref,
                     q1_ref, q2_ref, kr_ref, f_ref, carry_sc):
    tm = dq_ref.shape[0]

    @pl.when(pl.program_id(1) == 0)
    def _():
        carry_sc[...] = jnp.zeros_like(carry_sc)

    cos_a = ca_ref[...]
    sin_m = sm_ref[...]
    sin_p = sp_ref[...]
    lane = lax.broadcasted_iota(jnp.int32, (tm, LANES), 1)
    first_map = lane < DIFF_QK
    for h in range(N_HEADS):
        hs = slice(h * LANES, (h + 1) * LANES)
        for src_ref, is_q in ((dq_ref, True), (dk_ref, False)):
            x = src_ref[:, hs].astype(F32)
            rot = (x * cos_a + pltpu.roll(x, LANES - ROPE_DIMS // 2, axis=1) * sin_m
                   + pltpu.roll(x, ROPE_DIMS // 2, axis=1) * sin_p)
            if is_q:
                q1_ref[:, hs] = jnp.where(first_map, rot, 0.0).astype(q1_ref.dtype)
                q2_ref[:, hs] = jnp.where(first_map, 0.0, rot).astype(q2_ref.dtype)
            else:
                kr_ref[:, hs] = rot.astype(kr_ref.dtype)

    lf = _log_sigmoid(gz_ref[...] + fb_ref[...])
    r = lax.broadcasted_iota(jnp.int32, (tm, tm), 0)
    c = lax.broadcasted_iota(jnp.int32, (tm, tm), 1)
    lower = (c <= r).astype(F32)
    fcum = jnp.dot(lower, lf, precision=HIGHEST, preferred_element_type=F32) + carry_sc[0:1, :]
    f_ref[...] = fcum * LOG2E
    carry_sc[...] = jnp.broadcast_to(fcum[tm - 1:tm, :], carry_sc.shape)


def odd_prep(z, gz, f_bias_row, cos_a, sin_m, sin_p, batch):
    t = z.shape[0]
    seq = t // batch
    tm = min(TM_PROJ, seq)
    ns = seq // tm
    w = MIX_WIDTH
    tok = lambda col: pl.BlockSpec((tm, w), lambda b, s, col=col: (b * ns + s, col))
    tab = pl.BlockSpec((tm, LANES), lambda b, s: (s, 0))
    out_tok = pl.BlockSpec((tm, w), lambda b, s: (b * ns + s, 0))
    return pl.pallas_call(
        _odd_prep_kernel,
        out_shape=(jax.ShapeDtypeStruct((t, w), BF16), jax.ShapeDtypeStruct((t, w), BF16),
                   jax.ShapeDtypeStruct((t, w), BF16), jax.ShapeDtypeStruct((t, LANES), F32)),
        grid=(batch, ns),
        in_specs=[tok(3), tok(4),
                  pl.BlockSpec((tm, LANES), lambda b, s: (b * ns + s, 0)),
                  pl.BlockSpec((1, LANES), lambda b, s: (0, 0)),
                  tab, tab, tab],
        out_specs=(out_tok, out_tok, out_tok, pl.BlockSpec((tm, LANES), lambda b, s: (b * ns + s, 0))),
        scratch_shapes=[pltpu.VMEM((8, LANES), F32)],
        compiler_params=_params(("parallel", "arbitrary")),
        name="odd_prep",
    )(z, z, gz, f_bias_row, cos_a, sin_m, sin_p)


def _softmax_chunk(q, k, v, fk, fq, m_ref, l_ref, acc_ref, mask):
    s = lax.dot_general(q, k, (((1,), (1,)), ((), ())), preferred_element_type=F32)
    if fk is not None:
        s = s - fk
    if mask is not None:
        s = jnp.where(mask, s, NEG_INF)
    row_max = jnp.max(s, axis=-1, keepdims=True)
    if fq is not None:
        row_max = row_max + fq
    m_prev = m_ref[...]
    m_new = jnp.maximum(m_prev, row_max)
    p = jnp.exp2(s - (m_new if fq is None else m_new - fq))
    alpha = jnp.exp2(m_prev - m_new)
    l_ref[...] = alpha * l_ref[...] + jnp.sum(p, axis=-1, keepdims=True)
    acc_ref[...] = alpha * acc_ref[...] + jnp.dot(p.astype(BF16), v, preferred_element_type=F32)
    m_ref[...] = m_new


def _causal_sweep(i, tq, ch, streams):
    for st in streams:
        st["m"][...] = jnp.full_like(st["m"], -1e30)
        st["l"][...] = jnp.zeros_like(st["l"])
        st["acc"][...] = jnp.zeros_like(st["acc"])

    def chunk(j, mask):
        start = pl.multiple_of(j * ch, ch)
        for st in streams:
            fk = st["fk"](j) if st["fk"] is not None else None
            _softmax_chunk(st["q"], st["k"](start), st["v"](start), fk, st["fq"], st["m"], st["l"], st["acc"], mask)

    def body(j, carry):
        chunk(j, None)
        return carry

    n_full = (i * tq) // ch
    lax.fori_loop(0, n_full, body, 0)
    r = lax.broadcasted_iota(jnp.int32, (tq, ch), 0)
    c = lax.broadcasted_iota(jnp.int32, (tq, ch), 1)
    chunk(n_full, (c - r) <= (i * tq - n_full * ch))


def _fox_attention_kernel(q_ref, k_ref, v_ref, fcol_ref, frow_ref, out_ref, *scratch, tq, ch):
    hp = pl.program_id(1)
    i = pl.program_id(2)
    lane = lax.broadcasted_iota(jnp.int32, (tq, LANES), 1)
    fcol = fcol_ref[...]
    streams = []
    for n in range(2):
        hs = slice(n * LANES, (n + 1) * LANES)
        streams.append(dict(
            q=q_ref[:, hs],
            k=lambda start, hs=hs: k_ref[pl.ds(start, ch), hs],
            v=lambda start, hs=hs: v_ref[pl.ds(start, ch), hs],
            fk=lambda j, n=n: frow_ref[n, j],
            fq=jnp.sum(jnp.where(lane == 2 * hp + n, fcol, 0.0), axis=-1, keepdims=True),
            m=scratch[3 * n], l=scratch[3 * n + 1], acc=scratch[3 * n + 2]))
    _causal_sweep(i, tq, ch, streams)
    for n, st in enumerate(streams):
        out_ref[:, n * LANES:(n + 1) * LANES] = (st["acc"][...] / st["l"][...]).astype(out_ref.dtype)


def _diff_attention_kernel(q1_ref, q2_ref, k_ref, v_ref, lam_ref, ng_ref, out_ref, *scratch, tq, ch, lambda_init):
    i = pl.program_id(2)
    streams = []
    for n, q_ref in enumerate((q1_ref, q2_ref)):
        streams.append(dict(
            q=q_ref[...],
            k=lambda start: k_ref[pl.ds(start, ch), :],
            v=lambda start: v_ref[pl.ds(start, ch), :],
            fk=None, fq=None,
            m=scratch[3 * n], l=scratch[3 * n + 1], acc=scratch[3 * n + 2]))
    _causal_sweep(i, tq, ch, streams)
    o1, o2 = (st["acc"][...] / st["l"][...] for st in streams)
    lam_p = lam_ref[...]
    lam = (jnp.exp(jnp.sum(lam_p[0:1, :] * lam_p[1:2, :], axis=-1, keepdims=True))
           - jnp.exp(jnp.sum(lam_p[2:3, :] * lam_p[3:4, :], axis=-1, keepdims=True)) + lambda_init)
    out_ref[...] = (_rms(o1 - lam * o2, ng_ref[...]) * (1.0 - lambda_init)).astype(out_ref.dtype)


def _attention_tiles(seq):
    tq = min(ATT_TQ, seq)
    ch = min(ATT_CHUNK, seq)
    assert seq % ch == 0 and ch % tq == 0
    return tq, ch


def _stream_scratch(tq):
    return [pltpu.VMEM((tq, 1), F32), pltpu.VMEM((tq, 1), F32), pltpu.VMEM((tq, LANES), F32)] * 2


def fox_attention(z, fcol, frow, batch):
    t = z.shape[0]
    seq = t // batch
    tq, ch = _attention_tiles(seq)
    nq = seq // tq
    pairs = N_HEADS // 2
    w2 = 2 * LANES
    return pl.pallas_call(
        functools.partial(_fox_attention_kernel, tq=tq, ch=ch),
        out_shape=jax.ShapeDtypeStruct((t, MIX_WIDTH), BF16),
        grid=(batch, pairs, nq),
        in_specs=[pl.BlockSpec((tq, w2), lambda b, hp, i: (b * nq + i, hp)),
                  pl.BlockSpec((seq, w2), lambda b, hp, i: (b, pairs + hp)),
                  pl.BlockSpec((seq, w2), lambda b, hp, i: (b, 2 * pairs + hp)),
                  pl.BlockSpec((tq, LANES), lambda b, hp, i: (b * nq + i, 0)),
                  pl.BlockSpec((2, seq // ch, 1, ch), lambda b, hp, i: (b * pairs + hp, 0, 0, 0))],
        out_specs=pl.BlockSpec((tq, w2), lambda b, hp, i: (b * nq + i, hp)),
        scratch_shapes=_stream_scratch(tq),
        compiler_params=_params(("parallel", "parallel", "arbitrary")),
        name="fox_attention",
    )(z, z, z, fcol, frow)


def diff_attention(q1, q2, kr, z, v_col0, lam_p, ng, lambda_init, batch):
    t = z.shape[0]
    seq = t // batch
    tq, ch = _attention_tiles(seq)
    nq = seq // tq
    qspec = pl.BlockSpec((tq, LANES), lambda b, h, i: (b * nq + i, h))
    return pl.pallas_call(
        functools.partial(_diff_attention_kernel, tq=tq, ch=ch, lambda_init=lambda_init),
        out_shape=jax.ShapeDtypeStruct((t, MIX_WIDTH), BF16),
        grid=(batch, N_HEADS, nq),
        in_specs=[qspec, qspec,
                  pl.BlockSpec((seq, LANES), lambda b, h, i: (b, h)),
                  pl.BlockSpec((seq, LANES), lambda b, h, i: (b, v_col0 + h)),
                  pl.BlockSpec(lam_p.shape, lambda b, h, i: (0, 0)),
                  pl.BlockSpec(ng.shape, lambda b, h, i: (0, 0))],
        out_specs=qspec,
        scratch_shapes=_stream_scratch(tq),
        compiler_params=_params(("parallel", "parallel", "arbitrary")),
        name="diff_attention",
    )(q1, q2, kr, z, lam_p, ng)


def _router_kernel(ya_ref, yb_ref, wa_ref, wb_ref, res_ref, g_ref, wr_ref, x_ref, hp_ref, meta_ref, cnt_ref,
                   carry_sc):
    tm = res_ref.shape[0]

    @pl.when(pl.program_id(0) == 0)
    def _():
        carry_sc[...] = jnp.zeros_like(carry_sc)

    x = (res_ref[...] + jnp.dot(ya_ref[...], wa_ref[...], preferred_element_type=F32)
         + jnp.dot(yb_ref[...], wb_ref[...], preferred_element_type=F32))
    x_ref[...] = x
    hn = _rms(x, g_ref[...])
    hp_ref[...] = hn
    lane = lax.broadcasted_iota(jnp.int32, (tm, LANES), 1)
    logits = jnp.dot(hn, wr_ref[...], precision=HIGHEST, preferred_element_type=F32)
    logits = jnp.where(lane < N_EXPERTS, logits, NEG_INF)
    m1 = jnp.max(logits, axis=-1, keepdims=True)
    i1 = jnp.min(jnp.where(logits == m1, lane, LANES), axis=-1, keepdims=True)
    rest = jnp.where(lane == i1, NEG_INF, logits)
    m2 = jnp.max(rest, axis=-1, keepdims=True)
    i2 = jnp.min(jnp.where(rest == m2, lane, LANES), axis=-1, keepdims=True)
    e2 = jnp.exp(m2 - m1)
    w1 = 1.0 / (1.0 + e2)
    w2 = e2 * w1

    sel1 = lane == i1
    sel2 = lane == i2
    onehot = jnp.where(sel1, 1.0, 0.0) + jnp.where(sel2, 1.0, 0.0)
    r = lax.broadcasted_iota(jnp.int32, (tm, tm), 0)
    c = lax.broadcasted_iota(jnp.int32, (tm, tm), 1)
    lower = (c <= r).astype(BF16)
    incl = jnp.dot(lower, onehot.astype(BF16), preferred_element_type=F32)
    before = incl - onehot + carry_sc[0:1, :]
    r1 = jnp.sum(jnp.where(sel1, before, 0.0), axis=-1, keepdims=True)
    r2 = jnp.sum(jnp.where(sel2, before, 0.0), axis=-1, keepdims=True)
    total = incl[tm - 1:tm, :] + carry_sc[0:1, :]
    carry_sc[...] = jnp.broadcast_to(total, carry_sc.shape)
    cnt_ref[...] = jnp.broadcast_to(total, cnt_ref.shape)

    meta = jnp.where(lane == 0, i1.astype(F32), 0.0)
    meta = jnp.where(lane == 1, i2.astype(F32), meta)
    meta = jnp.where(lane == 2, r1, meta)
    meta = jnp.where(lane == 3, r2, meta)
    meta = jnp.where(lane == 4, w1, meta)
    meta = jnp.where(lane == 5, w2, meta)
    meta_ref[...] = meta


def moe_router(ya, yb, wa, wb, res, g, w_router_pad):
    t, d = res.shape
    tm = min(TM_PROJ, t)
    tok = lambda width: pl.BlockSpec((tm, width), lambda i: (i, 0))
    full = lambda a: pl.BlockSpec(a.shape, lambda i: (0, 0))
    return pl.pallas_call(
        _router_kernel,
        out_shape=(jax.ShapeDtypeStruct((t, d), F32), jax.ShapeDtypeStruct((t, d), F32),
                   jax.ShapeDtypeStruct((t, LANES), F32), jax.ShapeDtypeStruct((8, LANES), F32)),
        grid=(t // tm,),
        in_specs=[tok(ya.shape[1]), tok(yb.shape[1]), full(wa), full(wb), tok(d), full(g), full(w_router_pad)],
        out_specs=(tok(d), tok(d), tok(LANES), pl.BlockSpec((8, LANES), lambda i: (0, 0))),
        scratch_shapes=[pltpu.VMEM((8, LANES), F32)],
        compiler_params=_params(("arbitrary",)),
        name="moe_router",
    )(ya, yb, wa, wb, res, g, w_router_pad)


def _expert_ffn_kernel(te_ref, nt_ref, tok_ref, dst_ref, hn_ref, wgu_ref, wd_ref, planes_ref,
                       xbuf, ybuf, hbuf, gsem, ssem, zsem, *, d_ff, chunk, tm, n_pad_tiles):
    del te_ref
    i = pl.program_id(0)
    nt = nt_ref[0]
    n_real = planes_ref.shape[0] - (n_pad_tiles + 1) * tm

    def gather_row(tile, r):
        return pltpu.make_async_copy(hn_ref.at[pl.ds(tok_ref[tile * tm + r], 1)], xbuf.at[pl.ds(r, 1)], gsem)

    def scatter_row(tile, r):
        return pltpu.make_async_copy(ybuf.at[pl.ds(r, 1)], planes_ref.at[pl.ds(dst_ref[(tile + 1) * tm + r], 1)], ssem)

    def start_rows(make_copy, tile):
        def body(r, carry):
            make_copy(tile, r).start()
            return carry

        lax.fori_loop(0, tm, body, 0, unroll=8)

    def wait_gather():
        pltpu.make_async_copy(hn_ref.at[pl.ds(0, tm)], xbuf, gsem).wait()

    def wait_scatter():
        pltpu.make_async_copy(ybuf, planes_ref.at[pl.ds(0, tm)], ssem).wait()

    def zero_fill(p):
        return pltpu.make_async_copy(ybuf, planes_ref.at[pl.ds(n_real + p * tm, tm)], zsem)

    @pl.when(i == 0)
    def _():
        start_rows(gather_row, 0)
        ybuf[...] = jnp.zeros_like(ybuf)
        for p in range(n_pad_tiles):
            zero_fill(p).start()
        for p in range(n_pad_tiles):
            zero_fill(p).wait()

    @pl.when(i < nt)
    def _():
        wait_gather()
        hbuf[...] = xbuf[...].astype(BF16)

        def issue_rows(c, n_chunks):
            n_issue = n_chunks - 1
            if c < n_issue:
                for r in range(c * tm // n_issue, (c + 1) * tm // n_issue):
                    gather_row(i + 1, r).start()
                    scatter_row(i - 1, r).start()

        acc = _swiglu_accumulate(hbuf[...], wgu_ref.at[0], wd_ref.at[0], d_ff, chunk, after_chunk=issue_rows)
        wait_scatter()
        ybuf[...] = acc

    @pl.when(i == nt)
    def _():
        wait_gather()
        start_rows(scatter_row, i - 1)
        wait_scatter()


def moe_expert_ffn(tile_expert, n_tiles, row_token, row_dst, hn, w_gate_up, w_down):
    t, d = hn.shape
    n_exp, d_ff = w_down.shape[:2]
    tm = min(TM_MOE, 2 * t)
    n_grid = row_token.shape[0] // tm
    chunk = FF_CHUNK_MOE if d_ff % FF_CHUNK_MOE == 0 else d_ff
    w_index = lambda i, te, nt, tok, dst: (te[i], 0, 0)
    return pl.pallas_call(
        functools.partial(_expert_ffn_kernel, d_ff=d_ff, chunk=chunk, tm=tm, n_pad_tiles=n_exp),
        out_shape=jax.ShapeDtypeStruct((2 * t + (n_exp + 1) * tm, d), F32),
        grid_spec=pltpu.PrefetchScalarGridSpec(
            num_scalar_prefetch=4,
            grid=(n_grid,),
            in_specs=[pl.BlockSpec(memory_space=pl.ANY),
                      pl.BlockSpec((1, d, 2 * d_ff), w_index),
                      pl.BlockSpec((1, d_ff, d), w_index)],
            out_specs=pl.BlockSpec(memory_space=pl.ANY),
            scratch_shapes=[pltpu.VMEM((tm, d), F32), pltpu.VMEM((tm, d), F32), pltpu.VMEM((tm, d), BF16),
                            pltpu.SemaphoreType.DMA(()), pltpu.SemaphoreType.DMA(()),
                            pltpu.SemaphoreType.DMA(())]),
        compiler_params=_params(("arbitrary",), VMEM_LIMIT_MOE),
        name="moe_expert_ffn",
    )(tile_expert, n_tiles, row_token, row_dst, hn, w_gate_up, w_down)


def _combine_kernel(x_ref, meta_ref, g_ref, p0_ref, p1_ref, out_ref):
    meta = meta_ref[...]
    y = x_ref[...] + meta[:, 4:5] * p0_ref[...] + meta[:, 5:6] * p1_ref[...]
    out_ref[...] = _rms(y, g_ref[...])


def moe_combine(x, meta, g_final, planes):
    t, d = x.shape
    tm = min(TM_PROJ, t)
    nb = t // tm
    return pl.pallas_call(
        _combine_kernel,
        out_shape=jax.ShapeDtypeStruct((t, d), F32),
        grid=(nb,),
        in_specs=[pl.BlockSpec((tm, d), lambda i: (i, 0)),
                  pl.BlockSpec((tm, LANES), lambda i: (i, 0)),
                  pl.BlockSpec((1, d), lambda i: (0, 0)),
                  pl.BlockSpec((tm, d), lambda i: (i, 0)),
                  pl.BlockSpec((tm, d), lambda i: (nb + i, 0))],
        out_specs=pl.BlockSpec((tm, d), lambda i: (i, 0)),
        compiler_params=_params(("parallel",)),
        name="moe_combine",
    )(x, meta, g_final, planes, planes)


def moe_layer(ya, yb, wa, wb, res, g_ffn, w_router, w_gate_up, w_down, g_final):
    t, d = res.shape
    n_exp = w_router.shape[1]
    tm = min(TM_MOE, 2 * t)
    n_rows = 2 * t + n_exp * tm
    wr_pad = jnp.zeros((d, LANES), F32).at[:, :n_exp].set(w_router)
    x, hn, meta, counts = moe_router(ya, yb, wa, wb, res, g_ffn, wr_pad)

    cnt = counts[0, :n_exp].astype(jnp.int32)
    padded = ((cnt + tm - 1) // tm) * tm
    ends = jnp.cumsum(padded)
    starts = ends - padded
    e_idx = meta[:, 0:2].astype(jnp.int32)
    rank = meta[:, 2:4].astype(jnp.int32)
    start_of = jnp.sum(jnp.where(e_idx[..., None] == jnp.arange(n_exp), starts, 0), axis=-1)
    pos = (start_of + rank).T.reshape(-1)
    tile_start = jnp.arange(n_rows // tm, dtype=jnp.int32) * tm
    tile_expert = jnp.minimum(jnp.sum(tile_start[:, None] >= ends[None, :], axis=-1), n_exp - 1).astype(jnp.int32)
    n_tiles = (ends[-1:] // tm).astype(jnp.int32)
    pad_dst = (2 * t + tile_expert[:, None] * tm + jnp.arange(tm, dtype=jnp.int32)[None, :]).reshape(-1)
    served = jnp.full((n_rows,), -1, jnp.int32).at[pos].set(jnp.arange(2 * t, dtype=jnp.int32), unique_indices=True)
    zero_tile_dst = 2 * t + n_exp * tm + jnp.arange(tm, dtype=jnp.int32)
    row_dst = jnp.concatenate([zero_tile_dst, jnp.where(served >= 0, served, pad_dst)])
    row_token = jnp.where(served >= 0, served % t, 0)

    planes = moe_expert_ffn(tile_expert, n_tiles, row_token, row_dst, hn, w_gate_up, w_down)
    return moe_combine(x, meta, g_final, planes)


def _rope_tables(seq):
    half = ROPE_DIMS // 2
    inv = ROPE_THETA ** (-jnp.arange(half, dtype=F32) / half)
    ang = jnp.arange(seq, dtype=F32)[:, None] * inv[None, :]
    cos, sin = jnp.cos(ang), jnp.sin(ang)
    ones = jnp.ones((seq, DIFF_QK - ROPE_DIMS), F32)
    zeros = jnp.zeros((seq, DIFF_QK - ROPE_DIMS), F32)
    zh = jnp.zeros((seq, half), F32)
    cos_a = jnp.concatenate([cos, cos, ones], axis=1)
    sin_m = jnp.concatenate([-sin, zh, zeros], axis=1)
    sin_p = jnp.concatenate([zh, sin, zeros], axis=1)
    tile2 = lambda a: jnp.concatenate([a, a], axis=1)
    return tile2(cos_a), tile2(sin_m), tile2(sin_p)


def even_layer(x, batch, norm_mix, w_in, conv_w, conv_b, gate_b, ml_norm_g, sg_norm_g, sg_w, sg_b, w_out,
               norm_ffn, w_gate_up, w_down):
    w = MIX_WIDTH
    n_gate = 2 * N_HEADS
    w_main = jnp.concatenate([w_in[:, :4 * w], w_in[:, 4 * w + n_gate:]], axis=1).astype(BF16)
    w_gate = jnp.zeros((w_in.shape[0], LANES), F32).at[:, :n_gate].set(w_in[:, 4 * w:4 * w + n_gate]).astype(BF16)
    z, gz = norm_proj(x, norm_mix[None, :], w_main, w_gate)
    gates_t = gz[:, :n_gate].T
    y = even_mixer_core(z, gates_t, gate_b.reshape(n_gate, 1), conv_w, conv_b[None, :], ml_norm_g[None, :],
                        sg_norm_g[None, :], sg_w, sg_b.T, batch)
    return proj_dense_ffn(y, w_out.astype(BF16), x, norm_ffn[None, :], w_gate_up.astype(BF16), w_down.astype(BF16))


def odd_mixer_layer(x, batch, norm_mix, w_in, fox_f_b, diff_lambda, diff_norm_g, w_out, lambda_init):
    w = MIX_WIDTH
    t = x.shape[0]
    seq = t // batch
    w_main = jnp.concatenate([w_in[:, :w] * (LOG2E * HEAD_DIM ** -0.5), w_in[:, w:3 * w],
                              w_in[:, 3 * w + N_HEADS:4 * w + N_HEADS] * (LOG2E * DIFF_QK ** -0.5),
                              w_in[:, 4 * w + N_HEADS:]], axis=1).astype(BF16)
    w_gate = jnp.zeros((w_in.shape[0], LANES), F32).at[:, :N_HEADS].set(w_in[:, 3 * w:3 * w + N_HEADS]).astype(BF16)
    z, gz = norm_proj(x, norm_mix[None, :], w_main, w_gate)
    f_bias_row = jnp.zeros((1, LANES), F32).at[0, :N_HEADS].set(fox_f_b)
    cos_a, sin_m, sin_p = _rope_tables(seq)
    q1, q2, kr, fcol = odd_prep(z, gz, f_bias_row, cos_a, sin_m, sin_p, batch)
    _, ch = _attention_tiles(seq)
    frow = fcol[:, :N_HEADS].reshape(batch, seq, N_HEADS).transpose(0, 2, 1).reshape(
        batch * N_HEADS, seq // ch, 1, ch)
    y_fox = fox_attention(z, fcol, frow, batch)
    y_diff = diff_attention(q1, q2, kr, z, 5 * N_HEADS, diff_lambda, diff_norm_g[None, :], lambda_init, batch)
    w_out16 = w_out.astype(BF16)
    return y_fox, y_diff, w_out16[:w], w_out16[w:]


def kernel(x, even_norm_mix, even_w_in, even_ml_conv_w, even_ml_conv_b, even_ml_gate_b, even_ml_norm_g,
           even_sg_norm_g, even_sg_w, even_sg_b, even_w_out, even_norm_ffn, ffn_w_gate_up, ffn_w_down,
           odd_norm_mix, odd_w_in, odd_fox_f_b, odd_diff_lambda, odd_diff_norm_g, odd_w_out, odd_norm_ffn,
           moe_w_router, moe_w_gate_up, moe_w_down, final_norm):
    batch, seq, d = x.shape
    h = x.reshape(batch * seq, d)
    h = even_layer(h, batch, even_norm_mix[0], even_w_in[0], even_ml_conv_w[0], even_ml_conv_b[0],
                   even_ml_gate_b[0], even_ml_norm_g[0], even_sg_norm_g[0], even_sg_w[0], even_sg_b[0],
                   even_w_out[0], even_norm_ffn[0], ffn_w_gate_up[0], ffn_w_down[0])
    lambda_init = 0.8 - 0.6 * math.exp(-0.3 * 1)
    mixed = odd_mixer_layer(h, batch, odd_norm_mix[0], odd_w_in[0], odd_fox_f_b[0], odd_diff_lambda[0],
                            odd_diff_norm_g[0], odd_w_out[0], lambda_init)
    out = moe_layer(*mixed, h, odd_norm_ffn[0][None, :], moe_w_router[0], moe_w_gate_up[0].astype(BF16),
                    moe_w_down[0].astype(BF16), final_norm[None, :])
    return out.reshape(batch, seq, d)
```

```python
import functools
import math

import jax
import jax.numpy as jnp
from jax import lax
from jax.experimental import pallas as pl
from jax.experimental.pallas import tpu as pltpu

F32 = jnp.float32
BF16 = jnp.bfloat16
HIGHEST = lax.Precision.HIGHEST

EPS = 1e-6
CHUNK = 128
HEAD_DIM = 128
N_HEADS = 4
MIX_WIDTH = N_HEADS * HEAD_DIM
CONV_TAPS = 4
DIFF_QK = 64
ROPE_DIMS = DIFF_QK // 4
ROPE_THETA = 500000.0
N_EXPERTS = 8
LANES = 128
NEG_INF = float("-inf")

VMEM_LIMIT_DEFAULT = 48 * 1024 * 1024
VMEM_LIMIT_MOE = 60 * 1024 * 1024

TM_PROJ = 512
TM_MOE = 512
FF_CHUNK_DENSE = 256
FF_CHUNK_MOE = 512
ATT_TQ = 512
ATT_CHUNK = 2048
LOG2E = math.log2(math.e)


def _params(semantics, vmem=VMEM_LIMIT_DEFAULT):
    return pltpu.CompilerParams(dimension_semantics=semantics, vmem_limit_bytes=vmem)


def _rms(x, g):
    return x * lax.rsqrt(jnp.mean(x * x, axis=-1, keepdims=True) + EPS) * g


def _sigmoid(x):
    return 1.0 / (1.0 + jnp.exp(-x))


def _log_sigmoid(x):
    return jnp.minimum(x, 0.0) - jnp.log1p(jnp.exp(-jnp.abs(x)))


def _gelu_tanh(x):
    return 0.5 * x * (1.0 + jnp.tanh(math.sqrt(2.0 / math.pi) * (x + 0.044715 * (x * x * x))))


def _norm_proj_kernel(x_ref, g_ref, w_ref, wg_ref, z_ref, gz_ref, *, col_chunk):
    hn = _rms(x_ref[...], g_ref[...]).astype(BF16)
    for c in range(w_ref.shape[1] // col_chunk):
        cols = slice(c * col_chunk, (c + 1) * col_chunk)
        z_ref[:, cols] = jnp.dot(hn, w_ref[:, cols], preferred_element_type=F32).astype(z_ref.dtype)
    gz_ref[...] = jnp.dot(hn, wg_ref[...], preferred_element_type=F32)


def norm_proj(x, g, w_main, w_gate):
    t, d = x.shape
    n = w_main.shape[1]
    tm = min(TM_PROJ, t)
    return pl.pallas_call(
        functools.partial(_norm_proj_kernel, col_chunk=512),
        out_shape=(jax.ShapeDtypeStruct((t, n), BF16), jax.ShapeDtypeStruct((t, LANES), F32)),
        grid=(t // tm,),
        in_specs=[
            pl.BlockSpec((tm, d), lambda i: (i, 0)),
            pl.BlockSpec((1, d), lambda i: (0, 0)),
            pl.BlockSpec((d, n), lambda i: (0, 0)),
            pl.BlockSpec((d, LANES), lambda i: (0, 0)),
        ],
        out_specs=(pl.BlockSpec((tm, n), lambda i: (i, 0)), pl.BlockSpec((tm, LANES), lambda i: (i, 0))),
        compiler_params=_params(("parallel",)),
        name="norm_proj",
    )(x, g, w_main, w_gate)


def _even_mixer_kernel(zq_ref, zk_ref, zv_ref, zo_ref, zu_ref, zs_ref, gt_ref, gb_ref, cw_ref, cb_ref,
                       mlg_ref, sgg_ref, sgw_ref, sgb_ref, y_ref,
                       c_sc, n_sc, m_sc, tail_sc):
    L = CHUNK

    @pl.when(pl.program_id(1) == 0)
    def _():
        c_sc[...] = jnp.zeros_like(c_sc)
        n_sc[...] = jnp.zeros_like(n_sc)
        m_sc[...] = jnp.zeros_like(m_sc)
        tail_sc[...] = jnp.zeros_like(tail_sc)

    qk_raw = jnp.concatenate([zq_ref[...], zk_ref[...]], axis=1).astype(F32)
    ext = jnp.concatenate([tail_sc[...], qk_raw], axis=0)
    conv = cb_ref[...] + cw_ref[CONV_TAPS - 1:CONV_TAPS, :] * qk_raw
    for j in range(CONV_TAPS - 1):
        lo = 8 - (CONV_TAPS - 1) + j
        conv = conv + cw_ref[j:j + 1, :] * ext[lo:lo + L, :]
    tail_sc[...] = qk_raw[L - 8:, :]
    qk = conv * _sigmoid(conv)
    q_all = qk[:, :MIX_WIDTH] * (HEAD_DIM ** -0.5)
    k_all = qk[:, MIX_WIDTH:]

    gates = gt_ref[...] + gb_ref[...]
    row = lax.broadcasted_iota(jnp.int32, (8, L), 0)
    lgate = jnp.where(row < N_HEADS, gates, _log_sigmoid(gates))
    src = lax.broadcasted_iota(jnp.int32, (L, L), 0)
    dst = lax.broadcasted_iota(jnp.int32, (L, L), 1)
    upper = (src <= dst).astype(F32)
    csum = jnp.dot(lgate, upper, precision=HIGHEST, preferred_element_type=F32)
    rows8 = jnp.where(row < N_HEADS, gates, csum)
    cols = jnp.concatenate([rows8, jnp.zeros((L - 8, L), F32)], axis=0).T
    causal = dst <= src

    for h in range(N_HEADS):
        hs = slice(h * HEAD_DIM, (h + 1) * HEAD_DIM)
        li_row = rows8[h:h + 1, :]
        b_row = rows8[N_HEADS + h:N_HEADS + h + 1, :]
        li_col = cols[:, h:h + 1]
        b_col = cols[:, N_HEADS + h:N_HEADS + h + 1]
        m_prev = m_sc[h][0:1, 0:1]
        c_prev = c_sc[h]
        n_prev = n_sc[h][0:1, :]

        qh = q_all[:, hs]
        kh = k_all[:, hs]
        vh = zv_ref[:, hs]
        qh16 = qh.astype(BF16)
        kh16 = kh.astype(BF16)

        dmat = jnp.where(causal, b_col - b_row + li_row, NEG_INF)
        inter = b_col + m_prev
        m_row = jnp.maximum(inter, jnp.max(dmat, axis=-1, keepdims=True))
        w_intra = jnp.exp(dmat - m_row)
        w_inter = jnp.exp(inter - m_row)
        s = lax.dot_general(qh16, kh16, (((1,), (1,)), ((), ())), preferred_element_type=F32) * w_intra
        cq = lax.dot_general(qh16, c_prev.astype(BF16), (((1,), (1,)), ((), ())), preferred_element_type=F32)
        num = jnp.dot(s.astype(BF16), vh, preferred_element_type=F32) + w_inter * cq
        den = jnp.sum(s, axis=-1, keepdims=True) + w_inter * jnp.sum(qh * n_prev, axis=-1, keepdims=True)
        hm = num / jnp.maximum(jnp.abs(den), jnp.exp(-m_row))

        b_last = b_row[:, L - 1:L]
        g_row = b_last - b_row + li_row
        m_new = jnp.maximum(b_last + m_prev, jnp.max(g_row, axis=-1, keepdims=True))
        w_col = jnp.exp(b_last - b_col + li_col - m_new)
        decay = jnp.exp(b_last + m_prev - m_new)
        vw = (vh.astype(F32) * w_col).astype(BF16)
        c_sc[h] = decay * c_prev + lax.dot_general(vw, kh16, (((0,), (0,)), ((), ())), preferred_element_type=F32)
        n_new = decay * n_prev + jnp.sum(kh * w_col, axis=0, keepdims=True)
        n_sc[h] = jnp.broadcast_to(n_new, (8, HEAD_DIM))
        m_sc[h] = jnp.broadcast_to(m_new, (8, LANES))

        hm = _rms(hm, mlg_ref[:, hs])
        y_ref[:, hs] = (_sigmoid(zo_ref[:, hs].astype(F32)) * hm).astype(y_ref.dtype)

    u = _gelu_tanh(zu_ref[...].astype(F32))
    vs = _rms(_gelu_tanh(zs_ref[...].astype(F32)), sgg_ref[...]).astype(BF16)
    for g in range(N_HEADS):
        gs = slice(g * HEAD_DIM, (g + 1) * HEAD_DIM)
        wg = jnp.where(causal, sgw_ref[g], 0.0).astype(BF16)
        mixed = jnp.dot(wg, vs[:, gs], preferred_element_type=F32) + sgb_ref[:, g:g + 1]
        y_ref[:, MIX_WIDTH + g * HEAD_DIM:MIX_WIDTH + (g + 1) * HEAD_DIM] = (u[:, gs] * mixed).astype(y_ref.dtype)


def even_mixer_core(z, gates_t, gate_bias_col, conv_w, conv_b, ml_norm_g, sg_norm_g, sg_w, sg_b_t, batch):
    t = z.shape[0]
    nc = t // batch // CHUNK
    w = MIX_WIDTH

    def zspec(col):
        return pl.BlockSpec((CHUNK, w), lambda b, c, col=col: (b * nc + c, col))

    full = lambda shape: pl.BlockSpec(shape, lambda b, c: (0,) * len(shape))
    return pl.pallas_call(
        _even_mixer_kernel,
        out_shape=jax.ShapeDtypeStruct((t, 2 * w), BF16),
        grid=(batch, nc),
        in_specs=[zspec(0), zspec(1), zspec(2), zspec(3), zspec(4), zspec(5),
                  pl.BlockSpec((8, CHUNK), lambda b, c: (0, b * nc + c)),
                  full((8, 1)), full((CONV_TAPS, 2 * w)), full((1, 2 * w)),
                  full((1, w)), full((1, w)), full((N_HEADS, CHUNK, CHUNK)), full((CHUNK, N_HEADS))],
        out_specs=pl.BlockSpec((CHUNK, 2 * w), lambda b, c: (b * nc + c, 0)),
        scratch_shapes=[pltpu.VMEM((N_HEADS, HEAD_DIM, HEAD_DIM), F32),
                        pltpu.VMEM((N_HEADS, 8, HEAD_DIM), F32),
                        pltpu.VMEM((N_HEADS, 8, LANES), F32),
                        pltpu.VMEM((8, 2 * w), F32)],
        compiler_params=_params(("parallel", "arbitrary")),
        name="even_mixer",
    )(z, z, z, z, z, z, gates_t, gate_bias_col, conv_w, conv_b, ml_norm_g, sg_norm_g, sg_w, sg_b_t)


def _swiglu_accumulate(hn, wgu_ref, wd_ref, d_ff, chunk, after_chunk=None):
    acc = None
    n_chunks = d_ff // chunk
    for c in range(n_chunks):
        g = jnp.dot(hn, wgu_ref[:, c * chunk:(c + 1) * chunk], preferred_element_type=F32)
        u = jnp.dot(hn, wgu_ref[:, d_ff + c * chunk:d_ff + (c + 1) * chunk], preferred_element_type=F32)
        a = (g * _sigmoid(g) * u).astype(BF16)
        part = jnp.dot(a, wd_ref[c * chunk:(c + 1) * chunk, :], preferred_element_type=F32)
        acc = part if acc is None else acc + part
        if after_chunk is not None:
            after_chunk(c, n_chunks)
    return acc


def _dense_ffn_kernel(y_ref, wo_ref, res_ref, g_ref, wgu_ref, wd_ref, out_ref, *, d_ff, chunk):
    x = res_ref[...] + jnp.dot(y_ref[...], wo_ref[...], preferred_element_type=F32)
    hn = _rms(x, g_ref[...]).astype(BF16)
    out_ref[...] = x + _swiglu_accumulate(hn, wgu_ref, wd_ref, d_ff, chunk)


def proj_dense_ffn(y, w_out, res, g, w_gate_up, w_down):
    t, d = res.shape
    d_ff = w_down.shape[0]
    tm = min(TM_PROJ, t)
    chunk = FF_CHUNK_DENSE if d_ff % FF_CHUNK_DENSE == 0 else d_ff
    return pl.pallas_call(
        functools.partial(_dense_ffn_kernel, d_ff=d_ff, chunk=chunk),
        out_shape=jax.ShapeDtypeStruct((t, d), F32),
        grid=(t // tm,),
        in_specs=[pl.BlockSpec((tm, y.shape[1]), lambda i: (i, 0)),
                  pl.BlockSpec(w_out.shape, lambda i: (0, 0)),
                  pl.BlockSpec((tm, d), lambda i: (i, 0)),
                  pl.BlockSpec((1, d), lambda i: (0, 0)),
                  pl.BlockSpec(w_gate_up.shape, lambda i: (0, 0)),
                  pl.BlockSpec(w_down.shape, lambda i: (0, 0))],
        out_specs=pl.BlockSpec((tm, d), lambda i: (i, 0)),
        compiler_params=_params(("parallel",), VMEM_LIMIT_MOE),
        name="dense_ffn",
    )(y, w_out, res, g, w_gate_up, w_down)


def _odd_prep_kernel(dq_ref, dk_ref, gz_ref, fb_ref, ca_ref, sm_ref, sp_ref,
                     q1_ref, q2_ref, kr_ref, f_ref, carry_sc):
    tm = dq_ref.shape[0]

    @pl.when(pl.program_id(1) == 0)
    def _():
        carry_sc[...] = jnp.zeros_like(carry_sc)

    cos_a = ca_ref[...]
    sin_m = sm_ref[...]
    sin_p = sp_ref[...]
    lane = lax.broadcasted_iota(jnp.int32, (tm, LANES), 1)
    first_map = lane < DIFF_QK
    for h in range(N_HEADS):
        hs = slice(h * LANES, (h + 1) * LANES)
        for src_ref, is_q in ((dq_ref, True), (dk_ref, False)):
            x = src_ref[:, hs].astype(F32)
            rot = (x * cos_a + pltpu.roll(x, LANES - ROPE_DIMS // 2, axis=1) * sin_m
                   + pltpu.roll(x, ROPE_DIMS // 2, axis=1) * sin_p)
            if is_q:
                q1_ref[:, hs] = jnp.where(first_map, rot, 0.0).astype(q1_ref.dtype)
                q2_ref[:, hs] = jnp.where(first_map, 0.0, rot).astype(q2_ref.dtype)
            else:
                kr_ref[:, hs] = rot.astype(kr_ref.dtype)

    lf = _log_sigmoid(gz_ref[...] + fb_ref[...])
    r = lax.broadcasted_iota(jnp.int32, (tm, tm), 0)
    c = lax.broadcasted_iota(jnp.int32, (tm, tm), 1)
    lower = (c <= r).astype(F32)
    fcum = jnp.dot(lower, lf, precision=HIGHEST, preferred_element_type=F32) + carry_sc[0:1, :]
    f_ref[...] = fcum * LOG2E
    carry_sc[...] = jnp.broadcast_to(fcum[tm - 1:tm, :], carry_sc.shape)


def odd_prep(z, gz, f_bias_row, cos_a, sin_m, sin_p, batch):
    t = z.shape[0]
    seq = t // batch
    tm = min(TM_PROJ, seq)
    ns = seq // tm
    w = MIX_WIDTH
    tok = lambda col: pl.BlockSpec((tm, w), lambda b, s, col=col: (b * ns + s, col))
    tab = pl.BlockSpec((tm, LANES), lambda b, s: (s, 0))
    out_tok = pl.BlockSpec((tm, w), lambda b, s: (b * ns + s, 0))
    return pl.pallas_call(
        _odd_prep_kernel,
        out_shape=(jax.ShapeDtypeStruct((t, w), BF16), jax.ShapeDtypeStruct((t, w), BF16),
                   jax.ShapeDtypeStruct((t, w), BF16), jax.ShapeDtypeStruct((t, LANES), F32)),
        grid=(batch, ns),
        in_specs=[tok(3), tok(4),
                  pl.BlockSpec((tm, LANES), lambda b, s: (b * ns + s, 0)),
                  pl.BlockSpec((1, LANES), lambda b, s: (0, 0)),
                  tab, tab, tab],
        out_specs=(out_tok, out_tok, out_tok, pl.BlockSpec((tm, LANES), lambda b, s: (b * ns + s, 0))),
        scratch_shapes=[pltpu.VMEM((8, LANES), F32)],
        compiler_params=_params(("parallel", "arbitrary")),
        name="odd_prep",
    )(z, z, gz, f_bias_row, cos_a, sin_m, sin_p)


def _softmax_chunk(q, k, v, fk, fq, m_ref, l_ref, acc_ref, mask):
    s = lax.dot_general(q, k, (((1,), (1,)), ((), ())), preferred_element_type=F32)
    if fk is not None:
        s = s - fk
    if mask is not None:
        s = jnp.where(mask, s, NEG_INF)
    row_max = jnp.max(s, axis=-1, keepdims=True)
    if fq is not None:
        row_max = row_max + fq
    m_prev = m_ref[...]
    m_new = jnp.maximum(m_prev, row_max)
    p = jnp.exp2(s - (m_new if fq is None else m_new - fq))
    alpha = jnp.exp2(m_prev - m_new)
    l_ref[...] = alpha * l_ref[...] + jnp.sum(p, axis=-1, keepdims=True)
    acc_ref[...] = alpha * acc_ref[...] + jnp.dot(p.astype(BF16), v, preferred_element_type=F32)
    m_ref[...] = m_new


def _causal_sweep(i, tq, ch, streams):
    for st in streams:
        st["m"][...] = jnp.full_like(st["m"], -1e30)
        st["l"][...] = jnp.zeros_like(st["l"])
        st["acc"][...] = jnp.zeros_like(st["acc"])

    def chunk(j, mask):
        start = pl.multiple_of(j * ch, ch)
        for st in streams:
            fk = st["fk"](j) if st["fk"] is not None else None
            _softmax_chunk(st["q"], st["k"](start), st["v"](start), fk, st["fq"], st["m"], st["l"], st["acc"], mask)

    def body(j, carry):
        chunk(j, None)
        return carry

    n_full = (i * tq) // ch
    lax.fori_loop(0, n_full, body, 0)
    r = lax.broadcasted_iota(jnp.int32, (tq, ch), 0)
    c = lax.broadcasted_iota(jnp.int32, (tq, ch), 1)
    chunk(n_full, (c - r) <= (i * tq - n_full * ch))


def _fox_attention_kernel(q_ref, k_ref, v_ref, fcol_ref, frow_ref, out_ref, *scratch, tq, ch):
    hp = pl.program_id(1)
    i = pl.program_id(2)
    lane = lax.broadcasted_iota(jnp.int32, (tq, LANES), 1)
    fcol = fcol_ref[...]
    streams = []
    for n in range(2):
        hs = slice(n * LANES, (n + 1) * LANES)
        streams.append(dict(
            q=q_ref[:, hs],
            k=lambda start, hs=hs: k_ref[pl.ds(start, ch), hs],
            v=lambda start, hs=hs: v_ref[pl.ds(start, ch), hs],
            fk=lambda j, n=n: frow_ref[n, j],
            fq=jnp.sum(jnp.where(lane == 2 * hp + n, fcol, 0.0), axis=-1, keepdims=True),
            m=scratch[3 * n], l=scratch[3 * n + 1], acc=scratch[3 * n + 2]))
    _causal_sweep(i, tq, ch, streams)
    for n, st in enumerate(streams):
        out_ref[:, n * LANES:(n + 1) * LANES] = (st["acc"][...] / st["l"][...]).astype(out_ref.dtype)


def _diff_attention_kernel(q1_ref, q2_ref, k_ref, v_ref, lam_ref, ng_ref, out_ref, *scratch, tq, ch, lambda_init):
    i = pl.program_id(2)
    streams = []
    for n, q_ref in enumerate((q1_ref, q2_ref)):
        streams.append(dict(
            q=q_ref[...],
            k=lambda start: k_ref[pl.ds(start, ch), :],
            v=lambda start: v_ref[pl.ds(start, ch), :],
            fk=None, fq=None,
            m=scratch[3 * n], l=scratch[3 * n + 1], acc=scratch[3 * n + 2]))
    _causal_sweep(i, tq, ch, streams)
    o1, o2 = (st["acc"][...] / st["l"][...] for st in streams)
    lam_p = lam_ref[...]
    lam = (jnp.exp(jnp.sum(lam_p[0:1, :] * lam_p[1:2, :], axis=-1, keepdims=True))
           - jnp.exp(jnp.sum(lam_p[2:3, :] * lam_p[3:4, :], axis=-1, keepdims=True)) + lambda_init)
    out_ref[...] = (_rms(o1 - lam * o2, ng_ref[...]) * (1.0 - lambda_init)).astype(out_ref.dtype)


def _attention_tiles(seq):
    tq = min(ATT_TQ, seq)
    ch = min(ATT_CHUNK, seq)
    assert seq % ch == 0 and ch % tq == 0
    return tq, ch


def _stream_scratch(tq):
    return [pltpu.VMEM((tq, 1), F32), pltpu.VMEM((tq, 1), F32), pltpu.VMEM((tq, LANES), F32)] * 2


def fox_attention(z, fcol, frow, batch):
    t = z.shape[0]
    seq = t // batch
    tq, ch = _attention_tiles(seq)
    nq = seq // tq
    pairs = N_HEADS // 2
    w2 = 2 * LANES
    return pl.pallas_call(
        functools.partial(_fox_attention_kernel, tq=tq, ch=ch),
        out_shape=jax.ShapeDtypeStruct((t, MIX_WIDTH), BF16),
        grid=(batch, pairs, nq),
        in_specs=[pl.BlockSpec((tq, w2), lambda b, hp, i: (b * nq + i, hp)),
                  pl.BlockSpec((seq, w2), lambda b, hp, i: (b, pairs + hp)),
                  pl.BlockSpec((seq, w2), lambda b, hp, i: (b, 2 * pairs + hp)),
                  pl.BlockSpec((tq, LANES), lambda b, hp, i: (b * nq + i, 0)),
                  pl.BlockSpec((2, seq // ch, 1, ch), lambda b, hp, i: (b * pairs + hp, 0, 0, 0))],
        out_specs=pl.BlockSpec((tq, w2), lambda b, hp, i: (b * nq + i, hp)),
        scratch_shapes=_stream_scratch(tq),
        compiler_params=_params(("parallel", "parallel", "arbitrary")),
        name="fox_attention",
    )(z, z, z, fcol, frow)


def diff_attention(q1, q2, kr, z, v_col0, lam_p, ng, lambda_init, batch):
    t = z.shape[0]
    seq = t // batch
    tq, ch = _attention_tiles(seq)
    nq = seq // tq
    qspec = pl.BlockSpec((tq, LANES), lambda b, h, i: (b * nq + i, h))
    return pl.pallas_call(
        functools.partial(_diff_attention_kernel, tq=tq, ch=ch, lambda_init=lambda_init),
        out_shape=jax.ShapeDtypeStruct((t, MIX_WIDTH), BF16),
        grid=(batch, N_HEADS, nq),
        in_specs=[qspec, qspec,
                  pl.BlockSpec((seq, LANES), lambda b, h, i: (b, h)),
                  pl.BlockSpec((seq, LANES), lambda b, h, i: (b, v_col0 + h)),
                  pl.BlockSpec(lam_p.shape, lambda b, h, i: (0, 0)),
                  pl.BlockSpec(ng.shape, lambda b, h, i: (0, 0))],
        out_specs=qspec,
        scratch_shapes=_stream_scratch(tq),
        compiler_params=_params(("parallel", "parallel", "arbitrary")),
        name="diff_attention",
    )(q1, q2, kr, z, lam_p, ng)


def _router_kernel(ya_ref, yb_ref, wa_ref, wb_ref, res_ref, g_ref, wr_ref, x_ref, hp_ref, meta_ref, cnt_ref,
                   carry_sc):
    tm = res_ref.shape[0]

    @pl.when(pl.program_id(0) == 0)
    def _():
        carry_sc[...] = jnp.zeros_like(carry_sc)

    x = (res_ref[...] + jnp.dot(ya_ref[...], wa_ref[...], preferred_element_type=F32)
         + jnp.dot(yb_ref[...], wb_ref[...], preferred_element_type=F32))
    x_ref[...] = x
    hn = _rms(x, g_ref[...])
    hp_ref[...] = hn
    lane = lax.broadcasted_iota(jnp.int32, (tm, LANES), 1)
    h_hi = hn.astype(BF16)
    h_lo = (hn - h_hi.astype(F32)).astype(BF16)
    wr = wr_ref[...]
    w_hi = wr.astype(BF16)
    w_lo = (wr - w_hi.astype(F32)).astype(BF16)
    logits = (jnp.dot(h_hi, w_hi, preferred_element_type=F32)
              + (jnp.dot(h_lo, w_hi, preferred_element_type=F32) + jnp.dot(h_hi, w_lo, preferred_element_type=F32)))
    logits = jnp.where(lane < N_EXPERTS, logits, NEG_INF)
    m1 = jnp.max(logits, axis=-1, keepdims=True)
    i1 = jnp.min(jnp.where(logits == m1, lane, LANES), axis=-1, keepdims=True)
    rest = jnp.where(lane == i1, NEG_INF, logits)
    m2 = jnp.max(rest, axis=-1, keepdims=True)
    i2 = jnp.min(jnp.where(rest == m2, lane, LANES), axis=-1, keepdims=True)
    e2 = jnp.exp(m2 - m1)
    w1 = 1.0 / (1.0 + e2)
    w2 = e2 * w1

    sel1 = lane == i1
    sel2 = lane == i2
    onehot = jnp.where(sel1, 1.0, 0.0) + jnp.where(sel2, 1.0, 0.0)
    r = lax.broadcasted_iota(jnp.int32, (tm, tm), 0)
    c = lax.broadcasted_iota(jnp.int32, (tm, tm), 1)
    lower = (c <= r).astype(BF16)
    incl = jnp.dot(lower, onehot.astype(BF16), preferred_element_type=F32)
    before = incl - onehot + carry_sc[0:1, :]
    r1 = jnp.sum(jnp.where(sel1, before, 0.0), axis=-1, keepdims=True)
    r2 = jnp.sum(jnp.where(sel2, before, 0.0), axis=-1, keepdims=True)
    total = incl[tm - 1:tm, :] + carry_sc[0:1, :]
    carry_sc[...] = jnp.broadcast_to(total, carry_sc.shape)
    cnt_ref[...] = jnp.broadcast_to(total, cnt_ref.shape)

    meta = jnp.where(lane == 0, i1.astype(F32), 0.0)
    meta = jnp.where(lane == 1, i2.astype(F32), meta)
    meta = jnp.where(lane == 2, r1, meta)
    meta = jnp.where(lane == 3, r2, meta)
    meta = jnp.where(lane == 4, w1, meta)
    meta = jnp.where(lane == 5, w2, meta)
    meta_ref[...] = meta


def moe_router(ya, yb, wa, wb, res, g, w_router_pad):
    t, d = res.shape
    tm = min(TM_PROJ, t)
    tok = lambda width: pl.BlockSpec((tm, width), lambda i: (i, 0))
    full = lambda a: pl.BlockSpec(a.shape, lambda i: (0, 0))
    return pl.pallas_call(
        _router_kernel,
        out_shape=(jax.ShapeDtypeStruct((t, d), F32), jax.ShapeDtypeStruct((t, d), F32),
                   jax.ShapeDtypeStruct((t, LANES), F32), jax.ShapeDtypeStruct((8, LANES), F32)),
        grid=(t // tm,),
        in_specs=[tok(ya.shape[1]), tok(yb.shape[1]), full(wa), full(wb), tok(d), full(g), full(w_router_pad)],
        out_specs=(tok(d), tok(d), tok(LANES), pl.BlockSpec((8, LANES), lambda i: (0, 0))),
        scratch_shapes=[pltpu.VMEM((8, LANES), F32)],
        compiler_params=_params(("arbitrary",)),
        name="moe_router",
    )(ya, yb, wa, wb, res, g, w_router_pad)


def _expert_ffn_kernel(te_ref, nt_ref, tok_ref, dst_ref, hn_ref, wgu_ref, wd_ref, planes_ref,
                       xbuf, ybuf, hbuf, gsem, ssem, zsem, *, d_ff, chunk, tm, n_pad_tiles):
    del te_ref
    i = pl.program_id(0)
    nt = nt_ref[0]
    n_real = planes_ref.shape[0] - (n_pad_tiles + 1) * tm

    def gather_row(tile, r):
        return pltpu.make_async_copy(hn_ref.at[pl.ds(tok_ref[tile * tm + r], 1)], xbuf.at[pl.ds(r, 1)], gsem)

    def scatter_row(tile, r):
        return pltpu.make_async_copy(ybuf.at[pl.ds(r, 1)], planes_ref.at[pl.ds(dst_ref[(tile + 1) * tm + r], 1)], ssem)

    def start_rows(make_copy, tile):
        def body(r, carry):
            make_copy(tile, r).start()
            return carry

        lax.fori_loop(0, tm, body, 0, unroll=8)

    def wait_gather():
        pltpu.make_async_copy(hn_ref.at[pl.ds(0, tm)], xbuf, gsem).wait()

    def wait_scatter():
        pltpu.make_async_copy(ybuf, planes_ref.at[pl.ds(0, tm)], ssem).wait()

    def zero_fill(p):
        return pltpu.make_async_copy(ybuf, planes_ref.at[pl.ds(n_real + p * tm, tm)], zsem)

    @pl.when(i == 0)
    def _():
        start_rows(gather_row, 0)
        ybuf[...] = jnp.zeros_like(ybuf)
        for p in range(n_pad_tiles):
            zero_fill(p).start()
        for p in range(n_pad_tiles):
            zero_fill(p).wait()

    @pl.when(i < nt)
    def _():
        wait_gather()
        hbuf[...] = xbuf[...].astype(BF16)

        def issue_rows(c, n_chunks):
            n_issue = n_chunks - 1
            if c < n_issue:
                for r in range(c * tm // n_issue, (c + 1) * tm // n_issue):
                    gather_row(i + 1, r).start()
                    scatter_row(i - 1, r).start()

        acc = _swiglu_accumulate(hbuf[...], wgu_ref.at[0], wd_ref.at[0], d_ff, chunk, after_chunk=issue_rows)
        wait_scatter()
        ybuf[...] = acc

    @pl.when(i == nt)
    def _():
        wait_gather()
        start_rows(scatter_row, i - 1)
        wait_scatter()


def moe_expert_ffn(tile_expert, n_tiles, row_token, row_dst, hn, w_gate_up, w_down):
    t, d = hn.shape
    n_exp, d_ff = w_down.shape[:2]
    tm = min(TM_MOE, 2 * t)
    n_grid = row_token.shape[0] // tm
    chunk = FF_CHUNK_MOE if d_ff % FF_CHUNK_MOE == 0 else d_ff
    w_index = lambda i, te, nt, tok, dst: (te[i], 0, 0)
    return pl.pallas_call(
        functools.partial(_expert_ffn_kernel, d_ff=d_ff, chunk=chunk, tm=tm, n_pad_tiles=n_exp),
        out_shape=jax.ShapeDtypeStruct((2 * t + (n_exp + 1) * tm, d), F32),
        grid_spec=pltpu.PrefetchScalarGridSpec(
            num_scalar_prefetch=4,
            grid=(n_grid,),
            in_specs=[pl.BlockSpec(memory_space=pl.ANY),
                      pl.BlockSpec((1, d, 2 * d_ff), w_index),
                      pl.BlockSpec((1, d_ff, d), w_index)],
            out_specs=pl.BlockSpec(memory_space=pl.ANY),
            scratch_shapes=[pltpu.VMEM((tm, d), F32), pltpu.VMEM((tm, d), F32), pltpu.VMEM((tm, d), BF16),
                            pltpu.SemaphoreType.DMA(()), pltpu.SemaphoreType.DMA(()),
                            pltpu.SemaphoreType.DMA(())]),
        compiler_params=_params(("arbitrary",), VMEM_LIMIT_MOE),
        name="moe_expert_ffn",
    )(tile_expert, n_tiles, row_token, row_dst, hn, w_gate_up, w_down)


def _combine_kernel(x_ref, meta_ref, g_ref, p0_ref, p1_ref, out_ref):
    meta = meta_ref[...]
    y = x_ref[...] + meta[:, 4:5] * p0_ref[...] + meta[:, 5:6] * p1_ref[...]
    out_ref[...] = _rms(y, g_ref[...])


def moe_combine(x, meta, g_final, planes):
    t, d = x.shape
    tm = min(TM_PROJ, t)
    nb = t // tm
    return pl.pallas_call(
        _combine_kernel,
        out_shape=jax.ShapeDtypeStruct((t, d), F32),
        grid=(nb,),
        in_specs=[pl.BlockSpec((tm, d), lambda i: (i, 0)),
                  pl.BlockSpec((tm, LANES), lambda i: (i, 0)),
                  pl.BlockSpec((1, d), lambda i: (0, 0)),
                  pl.BlockSpec((tm, d), lambda i: (i, 0)),
                  pl.BlockSpec((tm, d), lambda i: (nb + i, 0))],
        out_specs=pl.BlockSpec((tm, d), lambda i: (i, 0)),
        compiler_params=_params(("parallel",)),
        name="moe_combine",
    )(x, meta, g_final, planes, planes)


def moe_layer(ya, yb, wa, wb, res, g_ffn, w_router, w_gate_up, w_down, g_final):
    t, d = res.shape
    n_exp = w_router.shape[1]
    tm = min(TM_MOE, 2 * t)
    n_rows = 2 * t + n_exp * tm
    wr_pad = jnp.zeros((d, LANES), F32).at[:, :n_exp].set(w_router)
    x, hn, meta, counts = moe_router(ya, yb, wa, wb, res, g_ffn, wr_pad)

    cnt = counts[0, :n_exp].astype(jnp.int32)
    padded = ((cnt + tm - 1) // tm) * tm
    ends = jnp.cumsum(padded)
    starts = ends - padded
    e_idx = meta[:, 0:2].astype(jnp.int32)
    rank = meta[:, 2:4].astype(jnp.int32)
    start_of = jnp.sum(jnp.where(e_idx[..., None] == jnp.arange(n_exp), starts, 0), axis=-1)
    pos = (start_of + rank).T.reshape(-1)
    tile_start = jnp.arange(n_rows // tm, dtype=jnp.int32) * tm
    tile_expert = jnp.minimum(jnp.sum(tile_start[:, None] >= ends[None, :], axis=-1), n_exp - 1).astype(jnp.int32)
    n_tiles = (ends[-1:] // tm).astype(jnp.int32)
    pad_dst = (2 * t + tile_expert[:, None] * tm + jnp.arange(tm, dtype=jnp.int32)[None, :]).reshape(-1)
    served = jnp.full((n_rows,), -1, jnp.int32).at[pos].set(jnp.arange(2 * t, dtype=jnp.int32), unique_indices=True)
    zero_tile_dst = 2 * t + n_exp * tm + jnp.arange(tm, dtype=jnp.int32)
    row_dst = jnp.concatenate([zero_tile_dst, jnp.where(served >= 0, served, pad_dst)])
    row_token = jnp.where(served >= 0, served % t, 0)

    planes = moe_expert_ffn(tile_expert, n_tiles, row_token, row_dst, hn, w_gate_up, w_down)
    return moe_combine(x, meta, g_final, planes)


def _rope_tables(seq):
    half = ROPE_DIMS // 2
    inv = ROPE_THETA ** (-jnp.arange(half, dtype=F32) / half)
    ang = jnp.arange(seq, dtype=F32)[:, None] * inv[None, :]
    cos, sin = jnp.cos(ang), jnp.sin(ang)
    ones = jnp.ones((seq, DIFF_QK - ROPE_DIMS), F32)
    zeros = jnp.zeros((seq, DIFF_QK - ROPE_DIMS), F32)
    zh = jnp.zeros((seq, half), F32)
    cos_a = jnp.concatenate([cos, cos, ones], axis=1)
    sin_m = jnp.concatenate([-sin, zh, zeros], axis=1)
    sin_p = jnp.concatenate([zh, sin, zeros], axis=1)
    tile2 = lambda a: jnp.concatenate([a, a], axis=1)
    return tile2(cos_a), tile2(sin_m), tile2(sin_p)


def even_layer(x, batch, norm_mix, w_in, conv_w, conv_b, gate_b, ml_norm_g, sg_norm_g, sg_w, sg_b, w_out,
               norm_ffn, w_gate_up, w_down):
    w = MIX_WIDTH
    n_gate = 2 * N_HEADS
    w_main = jnp.concatenate([w_in[:, :4 * w].astype(BF16), w_in[:, 4 * w + n_gate:].astype(BF16)], axis=1)
    w_gate = jnp.zeros((w_in.shape[0], LANES), F32).at[:, :n_gate].set(w_in[:, 4 * w:4 * w + n_gate]).astype(BF16)
    z, gz = norm_proj(x, norm_mix[None, :], w_main, w_gate)
    gates_t = gz[:, :n_gate].T
    y = even_mixer_core(z, gates_t, gate_b.reshape(n_gate, 1), conv_w, conv_b[None, :], ml_norm_g[None, :],
                        sg_norm_g[None, :], sg_w, sg_b.T, batch)
    return proj_dense_ffn(y, w_out.astype(BF16), x, norm_ffn[None, :], w_gate_up.astype(BF16), w_down.astype(BF16))


def odd_mixer_layer(x, batch, norm_mix, w_in, fox_f_b, diff_lambda, diff_norm_g, w_out, lambda_init):
    w = MIX_WIDTH
    t = x.shape[0]
    seq = t // batch
    w_main = jnp.concatenate([(w_in[:, :w] * (LOG2E * HEAD_DIM ** -0.5)).astype(BF16),
                              w_in[:, w:3 * w].astype(BF16),
                              (w_in[:, 3 * w + N_HEADS:4 * w + N_HEADS] * (LOG2E * DIFF_QK ** -0.5)).astype(BF16),
                              w_in[:, 4 * w + N_HEADS:].astype(BF16)], axis=1)
    w_gate = jnp.zeros((w_in.shape[0], LANES), F32).at[:, :N_HEADS].set(w_in[:, 3 * w:3 * w + N_HEADS]).astype(BF16)
    z, gz = norm_proj(x, norm_mix[None, :], w_main, w_gate)
    f_bias_row = jnp.zeros((1, LANES), F32).at[0, :N_HEADS].set(fox_f_b)
    cos_a, sin_m, sin_p = _rope_tables(seq)
    q1, q2, kr, fcol = odd_prep(z, gz, f_bias_row, cos_a, sin_m, sin_p, batch)
    _, ch = _attention_tiles(seq)
    frow = fcol[:, :N_HEADS].reshape(batch, seq, N_HEADS).transpose(0, 2, 1).reshape(
        batch * N_HEADS, seq // ch, 1, ch)
    y_fox = fox_attention(z, fcol, frow, batch)
    y_diff = diff_attention(q1, q2, kr, z, 5 * N_HEADS, diff_lambda, diff_norm_g[None, :], lambda_init, batch)
    w_out16 = w_out.astype(BF16)
    return y_fox, y_diff, w_out16[:w], w_out16[w:]


def kernel(x, even_norm_mix, even_w_in, even_ml_conv_w, even_ml_conv_b, even_ml_gate_b, even_ml_norm_g,
           even_sg_norm_g, even_sg_w, even_sg_b, even_w_out, even_norm_ffn, ffn_w_gate_up, ffn_w_down,
           odd_norm_mix, odd_w_in, odd_fox_f_b, odd_diff_lambda, odd_diff_norm_g, odd_w_out, odd_norm_ffn,
           moe_w_router, moe_w_gate_up, moe_w_down, final_norm):
    batch, seq, d = x.shape
    h = x.reshape(batch * seq, d)
    h = even_layer(h, batch, even_norm_mix[0], even_w_in[0], even_ml_conv_w[0], even_ml_conv_b[0],
                   even_ml_gate_b[0], even_ml_norm_g[0], even_sg_norm_g[0], even_sg_w[0], even_sg_b[0],
                   even_w_out[0], even_norm_ffn[0], ffn_w_gate_up[0], ffn_w_down[0])
    lambda_init = 0.8 - 0.6 * math.exp(-0.3 * 1)
    mixed = odd_mixer_layer(h, batch, odd_norm_mix[0], odd_w_in[0], odd_fox_f_b[0], odd_diff_lambda[0],
                            odd_diff_norm_g[0], odd_w_out[0], lambda_init)
    out = moe_layer(*mixed, h, odd_norm_ffn[0][None, :], moe_w_router[0], moe_w_gate_up[0].astype(BF16),
                    moe_w_down[0].astype(BF16), final_norm[None, :])
    return out.reshape(batch, seq, d)
```

```python
import functools
import math

import jax
import jax.numpy as jnp
from jax import lax
from jax.experimental import pallas as pl
from jax.experimental.pallas import tpu as pltpu

F32 = jnp.float32
BF16 = jnp.bfloat16
HIGHEST = lax.Precision.HIGHEST

EPS = 1e-6
CHUNK = 128
HEAD_DIM = 128
N_HEADS = 4
MIX_WIDTH = N_HEADS * HEAD_DIM
CONV_TAPS = 4
DIFF_QK = 64
ROPE_DIMS = DIFF_QK // 4
ROPE_THETA = 500000.0
N_EXPERTS = 8
LANES = 128
NEG_INF = float("-inf")

VMEM_LIMIT_DEFAULT = 48 * 1024 * 1024
VMEM_LIMIT_MOE = 60 * 1024 * 1024

TM_PROJ = 512
TM_MOE = 512
FF_CHUNK_DENSE = 256
FF_CHUNK_MOE = 512
ATT_TQ = 512
ATT_CHUNK = 2048
LOG2E = math.log2(math.e)


def _params(semantics, vmem=VMEM_LIMIT_DEFAULT):
    return pltpu.CompilerParams(dimension_semantics=semantics, vmem_limit_bytes=vmem)


def _rms(x, g):
    return x * lax.rsqrt(jnp.mean(x * x, axis=-1, keepdims=True) + EPS) * g


def _sigmoid(x):
    return 1.0 / (1.0 + jnp.exp(-x))


def _log_sigmoid(x):
    return jnp.minimum(x, 0.0) - jnp.log1p(jnp.exp(-jnp.abs(x)))


def _gelu_tanh(x):
    return 0.5 * x * (1.0 + jnp.tanh(math.sqrt(2.0 / math.pi) * (x + 0.044715 * (x * x * x))))


def _norm_proj_kernel(x_ref, g_ref, w_ref, wg_ref, z_ref, gz_ref, *, col_chunk):
    hn = _rms(x_ref[...], g_ref[...]).astype(BF16)
    for c in range(w_ref.shape[1] // col_chunk):
        cols = slice(c * col_chunk, (c + 1) * col_chunk)
        z_ref[:, cols] = jnp.dot(hn, w_ref[:, cols], preferred_element_type=F32).astype(z_ref.dtype)
    gz_ref[...] = jnp.dot(hn, wg_ref[...], preferred_element_type=F32)


def norm_proj(x, g, w_main, w_gate):
    t, d = x.shape
    n = w_main.shape[1]
    tm = min(TM_PROJ, t)
    return pl.pallas_call(
        functools.partial(_norm_proj_kernel, col_chunk=512),
        out_shape=(jax.ShapeDtypeStruct((t, n), BF16), jax.ShapeDtypeStruct((t, LANES), F32)),
        grid=(t // tm,),
        in_specs=[
            pl.BlockSpec((tm, d), lambda i: (i, 0)),
            pl.BlockSpec((1, d), lambda i: (0, 0)),
            pl.BlockSpec((d, n), lambda i: (0, 0)),
            pl.BlockSpec((d, LANES), lambda i: (0, 0)),
        ],
        out_specs=(pl.BlockSpec((tm, n), lambda i: (i, 0)), pl.BlockSpec((tm, LANES), lambda i: (i, 0))),
        compiler_params=_params(("parallel",)),
        name="norm_proj",
    )(x, g, w_main, w_gate)


def _even_mixer_kernel(zq_ref, zk_ref, zv_ref, zo_ref, zu_ref, zs_ref, gt_ref, gb_ref, cw_ref, cb_ref,
                       mlg_ref, sgg_ref, sgw_ref, sgb_ref, y_ref,
                       c_sc, n_sc, m_sc, tail_sc):
    L = CHUNK

    @pl.when(pl.program_id(1) == 0)
    def _():
        c_sc[...] = jnp.zeros_like(c_sc)
        n_sc[...] = jnp.zeros_like(n_sc)
        m_sc[...] = jnp.zeros_like(m_sc)
        tail_sc[...] = jnp.zeros_like(tail_sc)

    qk_raw = jnp.concatenate([zq_ref[...], zk_ref[...]], axis=1).astype(F32)
    ext = jnp.concatenate([tail_sc[...], qk_raw], axis=0)
    conv = cb_ref[...] + cw_ref[CONV_TAPS - 1:CONV_TAPS, :] * qk_raw
    for j in range(CONV_TAPS - 1):
        lo = 8 - (CONV_TAPS - 1) + j
        conv = conv + cw_ref[j:j + 1, :] * ext[lo:lo + L, :]
    tail_sc[...] = qk_raw[L - 8:, :]
    qk = conv * _sigmoid(conv)
    q_all = qk[:, :MIX_WIDTH] * (HEAD_DIM ** -0.5)
    k_all = qk[:, MIX_WIDTH:]

    gates = gt_ref[...] + gb_ref[...]
    row = lax.broadcasted_iota(jnp.int32, (8, L), 0)
    lgate = jnp.where(row < N_HEADS, gates, _log_sigmoid(gates))
    src = lax.broadcasted_iota(jnp.int32, (L, L), 0)
    dst = lax.broadcasted_iota(jnp.int32, (L, L), 1)
    upper = (src <= dst).astype(F32)
    csum = jnp.dot(lgate, upper, precision=HIGHEST, preferred_element_type=F32)
    rows8 = jnp.where(row < N_HEADS, gates, csum)
    cols = jnp.concatenate([rows8, jnp.zeros((L - 8, L), F32)], axis=0).T
    causal = dst <= src

    for h in range(N_HEADS):
        hs = slice(h * HEAD_DIM, (h + 1) * HEAD_DIM)
        li_row = rows8[h:h + 1, :]
        b_row = rows8[N_HEADS + h:N_HEADS + h + 1, :]
        li_col = cols[:, h:h + 1]
        b_col = cols[:, N_HEADS + h:N_HEADS + h + 1]
        m_prev = m_sc[h][0:1, 0:1]
        c_prev = c_sc[h]
        n_prev = n_sc[h][0:1, :]

        qh = q_all[:, hs]
        kh = k_all[:, hs]
        vh = zv_ref[:, hs]
        qh16 = qh.astype(BF16)
        kh16 = kh.astype(BF16)

        dmat = jnp.where(causal, b_col - b_row + li_row, NEG_INF)
        inter = b_col + m_prev
        m_row = jnp.maximum(inter, jnp.max(dmat, axis=-1, keepdims=True))
        w_intra = jnp.exp(dmat - m_row)
        w_inter = jnp.exp(inter - m_row)
        s = lax.dot_general(qh16, kh16, (((1,), (1,)), ((), ())), preferred_element_type=F32) * w_intra
        cq = lax.dot_general(qh16, c_prev.astype(BF16), (((1,), (1,)), ((), ())), preferred_element_type=F32)
        num = jnp.dot(s.astype(BF16), vh, preferred_element_type=F32) + w_inter * cq
        den = jnp.sum(s, axis=-1, keepdims=True) + w_inter * jnp.sum(qh * n_prev, axis=-1, keepdims=True)
        hm = num / jnp.maximum(jnp.abs(den), jnp.exp(-m_row))

        b_last = b_row[:, L - 1:L]
        g_row = b_last - b_row + li_row
        m_new = jnp.maximum(b_last + m_prev, jnp.max(g_row, axis=-1, keepdims=True))
        w_col = jnp.exp(b_last - b_col + li_col - m_new)
        decay = jnp.exp(b_last + m_prev - m_new)
        vw = (vh.astype(F32) * w_col).astype(BF16)
        c_sc[h] = decay * c_prev + lax.dot_general(vw, kh16, (((0,), (0,)), ((), ())), preferred_element_type=F32)
        n_new = decay * n_prev + jnp.sum(kh * w_col, axis=0, keepdims=True)
        n_sc[h] = jnp.broadcast_to(n_new, (8, HEAD_DIM))
        m_sc[h] = jnp.broadcast_to(m_new, (8, LANES))

        hm = _rms(hm, mlg_ref[:, hs])
        y_ref[:, hs] = (_sigmoid(zo_ref[:, hs].astype(F32)) * hm).astype(y_ref.dtype)

    u = _gelu_tanh(zu_ref[...].astype(F32))
    vs = _rms(_gelu_tanh(zs_ref[...].astype(F32)), sgg_ref[...]).astype(BF16)
    for g in range(N_HEADS):
        gs = slice(g * HEAD_DIM, (g + 1) * HEAD_DIM)
        wg = jnp.where(causal, sgw_ref[g], 0.0).astype(BF16)
        mixed = jnp.dot(wg, vs[:, gs], preferred_element_type=F32) + sgb_ref[:, g:g + 1]
        y_ref[:, MIX_WIDTH + g * HEAD_DIM:MIX_WIDTH + (g + 1) * HEAD_DIM] = (u[:, gs] * mixed).astype(y_ref.dtype)


def even_mixer_core(z, gates_t, gate_bias_col, conv_w, conv_b, ml_norm_g, sg_norm_g, sg_w, sg_b_t, batch):
    t = z.shape[0]
    nc = t // batch // CHUNK
    w = MIX_WIDTH

    def zspec(col):
        return pl.BlockSpec((CHUNK, w), lambda b, c, col=col: (b * nc + c, col))

    full = lambda shape: pl.BlockSpec(shape, lambda b, c: (0,) * len(shape))
    return pl.pallas_call(
        _even_mixer_kernel,
        out_shape=jax.ShapeDtypeStruct((t, 2 * w), BF16),
        grid=(batch, nc),
        in_specs=[zspec(0), zspec(1), zspec(2), zspec(3), zspec(4), zspec(5),
                  pl.BlockSpec((8, CHUNK), lambda b, c: (0, b * nc + c)),
                  full((8, 1)), full((CONV_TAPS, 2 * w)), full((1, 2 * w)),
                  full((1, w)), full((1, w)), full((N_HEADS, CHUNK, CHUNK)), full((CHUNK, N_HEADS))],
        out_specs=pl.BlockSpec((CHUNK, 2 * w), lambda b, c: (b * nc + c, 0)),
        scratch_shapes=[pltpu.VMEM((N_HEADS, HEAD_DIM, HEAD_DIM), F32),
                        pltpu.VMEM((N_HEADS, 8, HEAD_DIM), F32),
                        pltpu.VMEM((N_HEADS, 8, LANES), F32),
                        pltpu.VMEM((8, 2 * w), F32)],
        compiler_params=_params(("parallel", "arbitrary")),
        name="even_mixer",
    )(z, z, z, z, z, z, gates_t, gate_bias_col, conv_w, conv_b, ml_norm_g, sg_norm_g, sg_w, sg_b_t)


def _swiglu_accumulate(hn, wgu_ref, wd_ref, d_ff, chunk, after_chunk=None):
    acc = None
    n_chunks = d_ff // chunk
    for c in range(n_chunks):
        g = jnp.dot(hn, wgu_ref[:, c * chunk:(c + 1) * chunk], preferred_element_type=F32)
        u = jnp.dot(hn, wgu_ref[:, d_ff + c * chunk:d_ff + (c + 1) * chunk], preferred_element_type=F32)
        a = (g * _sigmoid(g) * u).astype(BF16)
        part = jnp.dot(a, wd_ref[c * chunk:(c + 1) * chunk, :], preferred_element_type=F32)
        acc = part if acc is None else acc + part
        if after_chunk is not None:
            after_chunk(c, n_chunks)
    return acc


def _dense_ffn_kernel(y_ref, wo_ref, res_ref, g_ref, wgu_ref, wd_ref, out_ref, *, d_ff, chunk):
    x = res_ref[...] + jnp.dot(y_ref[...], wo_ref[...], preferred_element_type=F32)
    hn = _rms(x, g_ref[...]).astype(BF16)
    out_ref[...] = x + _swiglu_accumulate(hn, wgu_ref, wd_ref, d_ff, chunk)


def proj_dense_ffn(y, w_out, res, g, w_gate_up, w_down):
    t, d = res.shape
    d_ff = w_down.shape[0]
    tm = min(TM_PROJ, t)
    chunk = FF_CHUNK_DENSE if d_ff % FF_CHUNK_DENSE == 0 else d_ff
    return pl.pallas_call(
        functools.partial(_dense_ffn_kernel, d_ff=d_ff, chunk=chunk),
        out_shape=jax.ShapeDtypeStruct((t, d), F32),
        grid=(t // tm,),
        in_specs=[pl.BlockSpec((tm, y.shape[1]), lambda i: (i, 0)),
                  pl.BlockSpec(w_out.shape, lambda i: (0, 0)),
                  pl.BlockSpec((tm, d), lambda i: (i, 0)),
                  pl.BlockSpec((1, d), lambda i: (0, 0)),
                  pl.BlockSpec(w_gate_up.shape, lambda i: (0, 0)),
                  pl.BlockSpec(w_down.shape, lambda i: (0, 0))],
        out_specs=pl.BlockSpec((tm, d), lambda i: (i, 0)),
        compiler_params=_params(("parallel",), VMEM_LIMIT_MOE),
        name="dense_ffn",
    )(y, w_out, res, g, w_gate_up, w_down)


def _odd_prep_kernel(dq_ref, dk_ref, gz_ref, fb_ref, ca_ref, sm_ref, sp_ref,
                     q1_ref, q2_ref, kr_ref, f_ref, carry_sc):
    tm = dq_ref.shape[0]

    @pl.when(pl.program_id(1) == 0)
    def _():
        carry_sc[...] = jnp.zeros_like(carry_sc)

    cos_a = ca_ref[...]
    sin_m = sm_ref[...]
    sin_p = sp_ref[...]
    lane = lax.broadcasted_iota(jnp.int32, (tm, LANES), 1)
    first_map = lane < DIFF_QK
    for h in range(N_HEADS):
        hs = slice(h * LANES, (h + 1) * LANES)
        for src_ref, is_q in ((dq_ref, True), (dk_ref, False)):
            x = src_ref[:, hs].astype(F32)
            rot = (x * cos_a + pltpu.roll(x, LANES - ROPE_DIMS // 2, axis=1) * sin_m
                   + pltpu.roll(x, ROPE_DIMS // 2, axis=1) * sin_p)
            if is_q:
                q1_ref[:, hs] = jnp.where(first_map, rot, 0.0).astype(q1_ref.dtype)
                q2_ref[:, hs] = jnp.where(first_map, 0.0, rot).astype(q2_ref.dtype)
            else:
                kr_ref[:, hs] = rot.astype(kr_ref.dtype)

    lf = _log_sigmoid(gz_ref[...] + fb_ref[...])
    r = lax.broadcasted_iota(jnp.int32, (tm, tm), 0)
    c = lax.broadcasted_iota(jnp.int32, (tm, tm), 1)
    lower = (c <= r).astype(F32)
    fcum = jnp.dot(lower, lf, precision=HIGHEST, preferred_element_type=F32) + carry_sc[0:1, :]
    f_ref[...] = fcum * LOG2E
    carry_sc[...] = jnp.broadcast_to(fcum[tm - 1:tm, :], carry_sc.shape)


def odd_prep(z, gz, f_bias_row, cos_a, sin_m, sin_p, batch):
    t = z.shape[0]
    seq = t // batch
    tm = min(TM_PROJ, seq)
    ns = seq // tm
    w = MIX_WIDTH
    tok = lambda col: pl.BlockSpec((tm, w), lambda b, s, col=col: (b * ns + s, col))
    tab = pl.BlockSpec((tm, LANES), lambda b, s: (s, 0))
    out_tok = pl.BlockSpec((tm, w), lambda b, s: (b * ns + s, 0))
    return pl.pallas_call(
        _odd_prep_kernel,
        out_shape=(jax.ShapeDtypeStruct((t, w), BF16), jax.ShapeDtypeStruct((t, w), BF16),
                   jax.ShapeDtypeStruct((t, w), BF16), jax.ShapeDtypeStruct((t, LANES), F32)),
        grid=(batch, ns),
        in_specs=[tok(3), tok(4),
                  pl.BlockSpec((tm, LANES), lambda b, s: (b * ns + s, 0)),
                  pl.BlockSpec((1, LANES), lambda b, s: (0, 0)),
                  tab, tab, tab],
        out_specs=(out_tok, out_tok, out_tok, pl.BlockSpec((tm, LANES), lambda b, s: (b * ns + s, 0))),
        scratch_shapes=[pltpu.VMEM((8, LANES), F32)],
        compiler_params=_params(("parallel", "arbitrary")),
        name="odd_prep",
    )(z, z, gz, f_bias_row, cos_a, sin_m, sin_p)


def _softmax_chunk(q, k, v, fk, fq, m_ref, l_ref, acc_ref, mask):
    s = lax.dot_general(q, k, (((1,), (1,)), ((), ())), preferred_element_type=F32)
    if fk is not None:
        s = s - fk
    if mask is not None:
        s = jnp.where(mask, s, NEG_INF)
    row_max = jnp.max(s, axis=-1, keepdims=True)
    if fq is not None:
        row_max = row_max + fq
    m_prev = m_ref[...]
    m_new = jnp.maximum(m_prev, row_max)
    p = jnp.exp2(s - (m_new if fq is None else m_new - fq))
    alpha = jnp.exp2(m_prev - m_new)
    l_ref[...] = alpha * l_ref[...] + jnp.sum(p, axis=-1, keepdims=True)
    acc_ref[...] = alpha * acc_ref[...] + jnp.dot(p.astype(BF16), v, preferred_element_type=F32)
    m_ref[...] = m_new


def _causal_sweep(i, tq, ch, streams):
    for st in streams:
        st["m"][...] = jnp.full_like(st["m"], -1e30)
        st["l"][...] = jnp.zeros_like(st["l"])
        st["acc"][...] = jnp.zeros_like(st["acc"])

    def chunk(j, mask):
        start = pl.multiple_of(j * ch, ch)
        for st in streams:
            fk = st["fk"](j) if st["fk"] is not None else None
            _softmax_chunk(st["q"], st["k"](start), st["v"](start), fk, st["fq"], st["m"], st["l"], st["acc"], mask)

    def body(j, carry):
        chunk(j, None)
        return carry

    n_full = (i * tq) // ch
    lax.fori_loop(0, n_full, body, 0)
    r = lax.broadcasted_iota(jnp.int32, (tq, ch), 0)
    c = lax.broadcasted_iota(jnp.int32, (tq, ch), 1)
    chunk(n_full, (c - r) <= (i * tq - n_full * ch))


def _fox_attention_kernel(q_ref, k_ref, v_ref, fcol_ref, frow_ref, out_ref, *scratch, tq, ch):
    hp = pl.program_id(1)
    i = pl.program_id(2)
    lane = lax.broadcasted_iota(jnp.int32, (tq, LANES), 1)
    fcol = fcol_ref[...]
    streams = []
    for n in range(2):
        hs = slice(n * LANES, (n + 1) * LANES)
        streams.append(dict(
            q=q_ref[:, hs],
            k=lambda start, hs=hs: k_ref[pl.ds(start, ch), hs],
            v=lambda start, hs=hs: v_ref[pl.ds(start, ch), hs],
            fk=lambda j, n=n: frow_ref[n, j],
            fq=jnp.sum(jnp.where(lane == 2 * hp + n, fcol, 0.0), axis=-1, keepdims=True),
            m=scratch[3 * n], l=scratch[3 * n + 1], acc=scratch[3 * n + 2]))
    _causal_sweep(i, tq, ch, streams)
    for n, st in enumerate(streams):
        out_ref[:, n * LANES:(n + 1) * LANES] = (st["acc"][...] / st["l"][...]).astype(out_ref.dtype)


def _diff_attention_kernel(q1_ref, q2_ref, k_ref, v_ref, lam_ref, ng_ref, out_ref, *scratch, tq, ch, lambda_init):
    i = pl.program_id(2)
    streams = []
    for n, q_ref in enumerate((q1_ref, q2_ref)):
        streams.append(dict(
            q=q_ref[...],
            k=lambda start: k_ref[pl.ds(start, ch), :],
            v=lambda start: v_ref[pl.ds(start, ch), :],
            fk=None, fq=None,
            m=scratch[3 * n], l=scratch[3 * n + 1], acc=scratch[3 * n + 2]))
    _causal_sweep(i, tq, ch, streams)
    o1, o2 = (st["acc"][...] / st["l"][...] for st in streams)
    lam_p = lam_ref[...]
    lam = (jnp.exp(jnp.sum(lam_p[0:1, :] * lam_p[1:2, :], axis=-1, keepdims=True))
           - jnp.exp(jnp.sum(lam_p[2:3, :] * lam_p[3:4, :], axis=-1, keepdims=True)) + lambda_init)
    out_ref[...] = (_rms(o1 - lam * o2, ng_ref[...]) * (1.0 - lambda_init)).astype(out_ref.dtype)


def _attention_tiles(seq):
    tq = min(ATT_TQ, seq)
    ch = min(ATT_CHUNK, seq)
    assert seq % ch == 0 and ch % tq == 0
    return tq, ch


def _stream_scratch(tq):
    return [pltpu.VMEM((tq, 1), F32), pltpu.VMEM((tq, 1), F32), pltpu.VMEM((tq, LANES), F32)] * 2


def fox_attention(z, fcol, frow, batch):
    t = z.shape[0]
    seq = t // batch
    tq, ch = _attention_tiles(seq)
    nq = seq // tq
    pairs = N_HEADS // 2
    w2 = 2 * LANES
    return pl.pallas_call(
        functools.partial(_fox_attention_kernel, tq=tq, ch=ch),
        out_shape=jax.ShapeDtypeStruct((t, MIX_WIDTH), BF16),
        grid=(batch, pairs, nq),
        in_specs=[pl.BlockSpec((tq, w2), lambda b, hp, i: (b * nq + i, hp)),
                  pl.BlockSpec((seq, w2), lambda b, hp, i: (b, pairs + hp)),
                  pl.BlockSpec((seq, w2), lambda b, hp, i: (b, 2 * pairs + hp)),
                  pl.BlockSpec((tq, LANES), lambda b, hp, i: (b * nq + i, 0)),
                  pl.BlockSpec((2, seq // ch, 1, ch), lambda b, hp, i: (b * pairs + hp, 0, 0, 0))],
        out_specs=pl.BlockSpec((tq, w2), lambda b, hp, i: (b * nq + i, hp)),
        scratch_shapes=_stream_scratch(tq),
        compiler_params=_params(("parallel", "parallel", "arbitrary")),
        name="fox_attention",
    )(z, z, z, fcol, frow)


def diff_attention(q1, q2, kr, z, v_col0, lam_p, ng, lambda_init, batch):
    t = z.shape[0]
    seq = t // batch
    tq, ch = _attention_tiles(seq)
    nq = seq // tq
    qspec = pl.BlockSpec((tq, LANES), lambda b, h, i: (b * nq + i, h))
    return pl.pallas_call(
        functools.partial(_diff_attention_kernel, tq=tq, ch=ch, lambda_init=lambda_init),
        out_shape=jax.ShapeDtypeStruct((t, MIX_WIDTH), BF16),
        grid=(batch, N_HEADS, nq),
        in_specs=[qspec, qspec,
                  pl.BlockSpec((seq, LANES), lambda b, h, i: (b, h)),
                  pl.BlockSpec((seq, LANES), lambda b, h, i: (b, v_col0 + h)),
                  pl.BlockSpec(lam_p.shape, lambda b, h, i: (0, 0)),
                  pl.BlockSpec(ng.shape, lambda b, h, i: (0, 0))],
        out_specs=qspec,
        scratch_shapes=_stream_scratch(tq),
        compiler_params=_params(("parallel", "parallel", "arbitrary")),
        name="diff_attention",
    )(q1, q2, kr, z, lam_p, ng)


def _router_kernel(ya_ref, yb_ref, wa_ref, wb_ref, res_ref, g_ref, wr_ref, x_ref, hp_ref, meta_ref, cnt_ref,
                   carry_sc):
    tm = res_ref.shape[0]

    @pl.when(pl.program_id(0) == 0)
    def _():
        carry_sc[...] = jnp.zeros_like(carry_sc)

    x = (res_ref[...] + jnp.dot(ya_ref[...], wa_ref[...], preferred_element_type=F32)
         + jnp.dot(yb_ref[...], wb_ref[...], preferred_element_type=F32))
    x_ref[...] = x
    hn = _rms(x, g_ref[...])
    hp_ref[...] = hn
    lane = lax.broadcasted_iota(jnp.int32, (tm, LANES), 1)
    h_hi = hn.astype(BF16)
    h_lo = (hn - h_hi.astype(F32)).astype(BF16)
    wr = wr_ref[...]
    w_hi = wr.astype(BF16)
    w_lo = (wr - w_hi.astype(F32)).astype(BF16)
    logits = (jnp.dot(h_hi, w_hi, preferred_element_type=F32)
              + (jnp.dot(h_lo, w_hi, preferred_element_type=F32) + jnp.dot(h_hi, w_lo, preferred_element_type=F32)))
    logits = jnp.where(lane < N_EXPERTS, logits, NEG_INF)
    m1 = jnp.max(logits, axis=-1, keepdims=True)
    i1 = jnp.min(jnp.where(logits == m1, lane, LANES), axis=-1, keepdims=True)
    rest = jnp.where(lane == i1, NEG_INF, logits)
    m2 = jnp.max(rest, axis=-1, keepdims=True)
    i2 = jnp.min(jnp.where(rest == m2, lane, LANES), axis=-1, keepdims=True)
    e2 = jnp.exp(m2 - m1)
    w1 = 1.0 / (1.0 + e2)
    w2 = e2 * w1

    sel1 = lane == i1
    sel2 = lane == i2
    onehot = jnp.where(sel1, 1.0, 0.0) + jnp.where(sel2, 1.0, 0.0)
    r = lax.broadcasted_iota(jnp.int32, (tm, tm), 0)
    c = lax.broadcasted_iota(jnp.int32, (tm, tm), 1)
    lower = (c <= r).astype(BF16)
    incl = jnp.dot(lower, onehot.astype(BF16), preferred_element_type=F32)
    before = incl - onehot + carry_sc[0:1, :]
    r1 = jnp.sum(jnp.where(sel1, before, 0.0), axis=-1, keepdims=True)
    r2 = jnp.sum(jnp.where(sel2, before, 0.0), axis=-1, keepdims=True)
    total = incl[tm - 1:tm, :] + carry_sc[0:1, :]
    carry_sc[...] = jnp.broadcast_to(total, carry_sc.shape)
    cnt_ref[...] = jnp.broadcast_to(total, cnt_ref.shape)

    meta = jnp.where(lane == 0, i1.astype(F32), 0.0)
    meta = jnp.where(lane == 1, i2.astype(F32), meta)
    meta = jnp.where(lane == 2, r1, meta)
    meta = jnp.where(lane == 3, r2, meta)
    meta = jnp.where(lane == 4, w1, meta)
    meta = jnp.where(lane == 5, w2, meta)
    meta_ref[...] = meta


def moe_router(ya, yb, wa, wb, res, g, w_router_pad):
    t, d = res.shape
    tm = min(TM_PROJ, t)
    tok = lambda width: pl.BlockSpec((tm, width), lambda i: (i, 0))
    full = lambda a: pl.BlockSpec(a.shape, lambda i: (0, 0))
    return pl.pallas_call(
        _router_kernel,
        out_shape=(jax.ShapeDtypeStruct((t, d), F32), jax.ShapeDtypeStruct((t, d), F32),
                   jax.ShapeDtypeStruct((t, LANES), F32), jax.ShapeDtypeStruct((8, LANES), F32)),
        grid=(t // tm,),
        in_specs=[tok(ya.shape[1]), tok(yb.shape[1]), full(wa), full(wb), tok(d), full(g), full(w_router_pad)],
        out_specs=(tok(d), tok(d), tok(LANES), pl.BlockSpec((8, LANES), lambda i: (0, 0))),
        scratch_shapes=[pltpu.VMEM((8, LANES), F32)],
        compiler_params=_params(("arbitrary",)),
        name="moe_router",
    )(ya, yb, wa, wb, res, g, w_router_pad)


def _expert_ffn_kernel(te_ref, nt_ref, tok_ref, dst_ref, hn_ref, wgu_ref, wd_ref, planes_ref,
                       xbuf, ybuf, hbuf, gsem, ssem, zsem, *, d_ff, chunk, tm, n_pad_tiles):
    del te_ref
    i = pl.program_id(0)
    nt = nt_ref[0]
    n_real = planes_ref.shape[0] - (n_pad_tiles + 1) * tm

    def gather_row(tile, r):
        return pltpu.make_async_copy(hn_ref.at[pl.ds(tok_ref[tile * tm + r], 1)], xbuf.at[pl.ds(r, 1)], gsem)

    def scatter_row(tile, r):
        return pltpu.make_async_copy(ybuf.at[pl.ds(r, 1)], planes_ref.at[pl.ds(dst_ref[(tile + 1) * tm + r], 1)], ssem)

    def start_rows(make_copy, tile):
        def body(r, carry):
            make_copy(tile, r).start()
            return carry

        lax.fori_loop(0, tm, body, 0, unroll=8)

    def wait_gather():
        pltpu.make_async_copy(hn_ref.at[pl.ds(0, tm)], xbuf, gsem).wait()

    def wait_scatter():
        pltpu.make_async_copy(ybuf, planes_ref.at[pl.ds(0, tm)], ssem).wait()

    def zero_fill(p):
        return pltpu.make_async_copy(ybuf, planes_ref.at[pl.ds(n_real + p * tm, tm)], zsem)

    @pl.when(i == 0)
    def _():
        start_rows(gather_row, 0)
        ybuf[...] = jnp.zeros_like(ybuf)
        for p in range(n_pad_tiles):
            zero_fill(p).start()
        for p in range(n_pad_tiles):
            zero_fill(p).wait()

    @pl.when(i < nt)
    def _():
        wait_gather()
        hbuf[...] = xbuf[...].astype(BF16)

        def issue_rows(c, n_chunks):
            n_issue = n_chunks - 1
            if c < n_issue:
                for r in range(c * tm // n_issue, (c + 1) * tm // n_issue):
                    gather_row(i + 1, r).start()
                    scatter_row(i - 1, r).start()

        acc = _swiglu_accumulate(hbuf[...], wgu_ref.at[0], wd_ref.at[0], d_ff, chunk, after_chunk=issue_rows)
        wait_scatter()
        ybuf[...] = acc

    @pl.when(i == nt)
    def _():
        wait_gather()
        start_rows(scatter_row, i - 1)
        wait_scatter()


def moe_expert_ffn(tile_expert, n_tiles, row_token, row_dst, hn, w_gate_up, w_down):
    t, d = hn.shape
    n_exp, d_ff = w_down.shape[:2]
    tm = min(TM_MOE, 2 * t)
    n_grid = row_token.shape[0] // tm
    chunk = FF_CHUNK_MOE if d_ff % FF_CHUNK_MOE == 0 else d_ff
    w_index = lambda i, te, nt, tok, dst: (te[i], 0, 0)
    return pl.pallas_call(
        functools.partial(_expert_ffn_kernel, d_ff=d_ff, chunk=chunk, tm=tm, n_pad_tiles=n_exp),
        out_shape=jax.ShapeDtypeStruct((2 * t + (n_exp + 1) * tm, d), F32),
        grid_spec=pltpu.PrefetchScalarGridSpec(
            num_scalar_prefetch=4,
            grid=(n_grid,),
            in_specs=[pl.BlockSpec(memory_space=pl.ANY),
                      pl.BlockSpec((1, d, 2 * d_ff), w_index),
                      pl.BlockSpec((1, d_ff, d), w_index)],
            out_specs=pl.BlockSpec(memory_space=pl.ANY),
            scratch_shapes=[pltpu.VMEM((tm, d), F32), pltpu.VMEM((tm, d), F32), pltpu.VMEM((tm, d), BF16),
                            pltpu.SemaphoreType.DMA(()), pltpu.SemaphoreType.DMA(()),
                            pltpu.SemaphoreType.DMA(())]),
        compiler_params=_params(("arbitrary",), VMEM_LIMIT_MOE),
        name="moe_expert_ffn",
    )(tile_expert, n_tiles, row_token, row_dst, hn, w_gate_up, w_down)


def _combine_kernel(x_ref, meta_ref, g_ref, p0_ref, p1_ref, out_ref):
    meta = meta_ref[...]
    y = x_ref[...] + meta[:, 4:5] * p0_ref[...] + meta[:, 5:6] * p1_ref[...]
    out_ref[...] = _rms(y, g_ref[...])


def moe_combine(x, meta, g_final, planes):
    t, d = x.shape
    tm = min(TM_PROJ, t)
    nb = t // tm
    return pl.pallas_call(
        _combine_kernel,
        out_shape=jax.ShapeDtypeStruct((t, d), F32),
        grid=(nb,),
        in_specs=[pl.BlockSpec((tm, d), lambda i: (i, 0)),
                  pl.BlockSpec((tm, LANES), lambda i: (i, 0)),
                  pl.BlockSpec((1, d), lambda i: (0, 0)),
                  pl.BlockSpec((tm, d), lambda i: (i, 0)),
                  pl.BlockSpec((tm, d), lambda i: (nb + i, 0))],
        out_specs=pl.BlockSpec((tm, d), lambda i: (i, 0)),
        compiler_params=_params(("parallel",)),
        name="moe_combine",
    )(x, meta, g_final, planes, planes)


def moe_layer(ya, yb, wa, wb, res, g_ffn, w_router, w_gate_up, w_down, g_final):
    t, d = res.shape
    n_exp = w_router.shape[1]
    tm = min(TM_MOE, 2 * t)
    n_rows = 2 * t + n_exp * tm
    wr_pad = jnp.zeros((d, LANES), F32).at[:, :n_exp].set(w_router)
    x, hn, meta, counts = moe_router(ya, yb, wa, wb, res, g_ffn, wr_pad)

    cnt = counts[0, :n_exp].astype(jnp.int32)
    padded = ((cnt + tm - 1) // tm) * tm
    ends = jnp.cumsum(padded)
    starts = ends - padded
    e_idx = meta[:, 0:2].astype(jnp.int32)
    rank = meta[:, 2:4].astype(jnp.int32)
    start_of = jnp.sum(jnp.where(e_idx[..., None] == jnp.arange(n_exp), starts, 0), axis=-1)
    pos = (start_of + rank).T.reshape(-1)
    tile_start = jnp.arange(n_rows // tm, dtype=jnp.int32) * tm
    tile_expert = jnp.minimum(jnp.sum(tile_start[:, None] >= ends[None, :], axis=-1), n_exp - 1).astype(jnp.int32)
    n_tiles = (ends[-1:] // tm).astype(jnp.int32)
    pad_dst = (2 * t + tile_expert[:, None] * tm + jnp.arange(tm, dtype=jnp.int32)[None, :]).reshape(-1)
    served = jnp.full((n_rows,), -1, jnp.int32).at[pos].set(jnp.arange(2 * t, dtype=jnp.int32), unique_indices=True,
                                                            mode="promise_in_bounds")
    zero_tile_dst = 2 * t + n_exp * tm + jnp.arange(tm, dtype=jnp.int32)
    row_dst = jnp.concatenate([zero_tile_dst, jnp.where(served >= 0, served, pad_dst)])
    row_token = jnp.where(served >= 0, served % t, 0)

    planes = moe_expert_ffn(tile_expert, n_tiles, row_token, row_dst, hn, w_gate_up, w_down)
    return moe_combine(x, meta, g_final, planes)


def _rope_tables(seq):
    half = ROPE_DIMS // 2
    inv = ROPE_THETA ** (-jnp.arange(half, dtype=F32) / half)
    ang = jnp.arange(seq, dtype=F32)[:, None] * inv[None, :]
    cos, sin = jnp.cos(ang), jnp.sin(ang)
    ones = jnp.ones((seq, DIFF_QK - ROPE_DIMS), F32)
    zeros = jnp.zeros((seq, DIFF_QK - ROPE_DIMS), F32)
    zh = jnp.zeros((seq, half), F32)
    cos_a = jnp.concatenate([cos, cos, ones], axis=1)
    sin_m = jnp.concatenate([-sin, zh, zeros], axis=1)
    sin_p = jnp.concatenate([zh, sin, zeros], axis=1)
    tile2 = lambda a: jnp.concatenate([a, a], axis=1)
    return tile2(cos_a), tile2(sin_m), tile2(sin_p)


def even_layer(x, batch, norm_mix, w_in, conv_w, conv_b, gate_b, ml_norm_g, sg_norm_g, sg_w, sg_b, w_out,
               norm_ffn, w_gate_up, w_down):
    w = MIX_WIDTH
    n_gate = 2 * N_HEADS
    w_main = jnp.concatenate([w_in[:, :4 * w].astype(BF16), w_in[:, 4 * w + n_gate:].astype(BF16)], axis=1)
    w_gate = jnp.zeros((w_in.shape[0], LANES), F32).at[:, :n_gate].set(w_in[:, 4 * w:4 * w + n_gate]).astype(BF16)
    z, gz = norm_proj(x, norm_mix[None, :], w_main, w_gate)
    gates_t = gz[:, :n_gate].T
    y = even_mixer_core(z, gates_t, gate_b.reshape(n_gate, 1), conv_w, conv_b[None, :], ml_norm_g[None, :],
                        sg_norm_g[None, :], sg_w, sg_b.T, batch)
    return proj_dense_ffn(y, w_out.astype(BF16), x, norm_ffn[None, :], w_gate_up.astype(BF16), w_down.astype(BF16))


def odd_mixer_layer(x, batch, norm_mix, w_in, fox_f_b, diff_lambda, diff_norm_g, w_out, lambda_init):
    w = MIX_WIDTH
    t = x.shape[0]
    seq = t // batch
    w_main = jnp.concatenate([(w_in[:, :w] * (LOG2E * HEAD_DIM ** -0.5)).astype(BF16),
                              w_in[:, w:3 * w].astype(BF16),
                              (w_in[:, 3 * w + N_HEADS:4 * w + N_HEADS] * (LOG2E * DIFF_QK ** -0.5)).astype(BF16),
                              w_in[:, 4 * w + N_HEADS:].astype(BF16)], axis=1)
    w_gate = jnp.zeros((w_in.shape[0], LANES), F32).at[:, :N_HEADS].set(w_in[:, 3 * w:3 * w + N_HEADS]).astype(BF16)
    z, gz = norm_proj(x, norm_mix[None, :], w_main, w_gate)
    f_bias_row = jnp.zeros((1, LANES), F32).at[0, :N_HEADS].set(fox_f_b)
    cos_a, sin_m, sin_p = _rope_tables(seq)
    q1, q2, kr, fcol = odd_prep(z, gz, f_bias_row, cos_a, sin_m, sin_p, batch)
    _, ch = _attention_tiles(seq)
    frow = fcol[:, :N_HEADS].reshape(batch, seq, N_HEADS).transpose(0, 2, 1).reshape(
        batch * N_HEADS, seq // ch, 1, ch)
    y_fox = fox_attention(z, fcol, frow, batch)
    y_diff = diff_attention(q1, q2, kr, z, 5 * N_HEADS, diff_lambda, diff_norm_g[None, :], lambda_init, batch)
    w_out16 = w_out.astype(BF16)
    return y_fox, y_diff, w_out16[:w], w_out16[w:]


def kernel(x, even_norm_mix, even_w_in, even_ml_conv_w, even_ml_conv_b, even_ml_gate_b, even_ml_norm_g,
           even_sg_norm_g, even_sg_w, even_sg_b, even_w_out, even_norm_ffn, ffn_w_gate_up, ffn_w_down,
           odd_norm_mix, odd_w_in, odd_fox_f_b, odd_diff_lambda, odd_diff_norm_g, odd_w_out, odd_norm_ffn,
           moe_w_router, moe_w_gate_up, moe_w_down, final_norm):
    batch, seq, d = x.shape
    h = x.reshape(batch * seq, d)
    h = even_layer(h, batch, even_norm_mix[0], even_w_in[0], even_ml_conv_w[0], even_ml_conv_b[0],
                   even_ml_gate_b[0], even_ml_norm_g[0], even_sg_norm_g[0], even_sg_w[0], even_sg_b[0],
                   even_w_out[0], even_norm_ffn[0], ffn_w_gate_up[0], ffn_w_down[0])
    lambda_init = 0.8 - 0.6 * math.exp(-0.3 * 1)
    mixed = odd_mixer_layer(h, batch, odd_norm_mix[0], odd_w_in[0], odd_fox_f_b[0], odd_diff_lambda[0],
                            odd_diff_norm_g[0], odd_w_out[0], lambda_init)
    out = moe_layer(*mixed, h, odd_norm_ffn[0][None, :], moe_w_router[0], moe_w_gate_up[0].astype(BF16),
                    moe_w_down[0].astype(BF16), final_norm[None, :])
    return out.reshape(batch, seq, d)
```

```python
import functools
import math

import jax
import jax.numpy as jnp
from jax import lax
from jax.experimental import pallas as pl
from jax.experimental.pallas import tpu as pltpu

F32 = jnp.float32
BF16 = jnp.bfloat16
HIGHEST = lax.Precision.HIGHEST

EPS = 1e-6
CHUNK = 128
HEAD_DIM = 128
N_HEADS = 4
MIX_WIDTH = N_HEADS * HEAD_DIM
CONV_TAPS = 4
DIFF_QK = 64
ROPE_DIMS = DIFF_QK // 4
ROPE_THETA = 500000.0
N_EXPERTS = 8
LANES = 128
NEG_INF = float("-inf")

VMEM_LIMIT_DEFAULT = 48 * 1024 * 1024
VMEM_LIMIT_MOE = 60 * 1024 * 1024

TM_PROJ = 512
TM_MOE = 512
FF_CHUNK_DENSE = 256
FF_CHUNK_MOE = 512
ATT_TQ = 1024
ATT_CHUNK = 2048
LOG2E = math.log2(math.e)


def _params(semantics, vmem=VMEM_LIMIT_DEFAULT):
    return pltpu.CompilerParams(dimension_semantics=semantics, vmem_limit_bytes=vmem)


def _rms(x, g):
    return x * lax.rsqrt(jnp.mean(x * x, axis=-1, keepdims=True) + EPS) * g


def _sigmoid(x):
    return 1.0 / (1.0 + jnp.exp(-x))


def _log_sigmoid(x):
    return jnp.minimum(x, 0.0) - jnp.log1p(jnp.exp(-jnp.abs(x)))


def _gelu_tanh(x):
    return 0.5 * x * (1.0 + jnp.tanh(math.sqrt(2.0 / math.pi) * (x + 0.044715 * (x * x * x))))


def _norm_proj_kernel(x_ref, g_ref, w_ref, wg_ref, z_ref, gz_ref, *, col_chunk):
    hn = _rms(x_ref[...], g_ref[...]).astype(BF16)
    for c in range(w_ref.shape[1] // col_chunk):
        cols = slice(c * col_chunk, (c + 1) * col_chunk)
        z_ref[:, cols] = jnp.dot(hn, w_ref[:, cols], preferred_element_type=F32).astype(z_ref.dtype)
    gz_ref[...] = jnp.dot(hn, wg_ref[...], preferred_element_type=F32)


def norm_proj(x, g, w_main, w_gate):
    t, d = x.shape
    n = w_main.shape[1]
    tm = min(TM_PROJ, t)
    return pl.pallas_call(
        functools.partial(_norm_proj_kernel, col_chunk=512),
        out_shape=(jax.ShapeDtypeStruct((t, n), BF16), jax.ShapeDtypeStruct((t, LANES), F32)),
        grid=(t // tm,),
        in_specs=[
            pl.BlockSpec((tm, d), lambda i: (i, 0)),
            pl.BlockSpec((1, d), lambda i: (0, 0)),
            pl.BlockSpec((d, n), lambda i: (0, 0)),
            pl.BlockSpec((d, LANES), lambda i: (0, 0)),
        ],
        out_specs=(pl.BlockSpec((tm, n), lambda i: (i, 0)), pl.BlockSpec((tm, LANES), lambda i: (i, 0))),
        compiler_params=_params(("parallel",)),
        name="norm_proj",
    )(x, g, w_main, w_gate)


def _even_mixer_kernel(zq_ref, zk_ref, zv_ref, zo_ref, zu_ref, zs_ref, gt_ref, gb_ref, cw_ref, cb_ref,
                       mlg_ref, sgg_ref, sgw_ref, sgb_ref, y_ref,
                       c_sc, n_sc, m_sc, tail_sc):
    L = CHUNK

    @pl.when(pl.program_id(1) == 0)
    def _():
        c_sc[...] = jnp.zeros_like(c_sc)
        n_sc[...] = jnp.zeros_like(n_sc)
        m_sc[...] = jnp.zeros_like(m_sc)
        tail_sc[...] = jnp.zeros_like(tail_sc)

    qk_raw = jnp.concatenate([zq_ref[...], zk_ref[...]], axis=1).astype(F32)
    ext = jnp.concatenate([tail_sc[...], qk_raw], axis=0)
    conv = cb_ref[...] + cw_ref[CONV_TAPS - 1:CONV_TAPS, :] * qk_raw
    for j in range(CONV_TAPS - 1):
        lo = 8 - (CONV_TAPS - 1) + j
        conv = conv + cw_ref[j:j + 1, :] * ext[lo:lo + L, :]
    tail_sc[...] = qk_raw[L - 8:, :]
    qk = conv * _sigmoid(conv)
    q_all = qk[:, :MIX_WIDTH] * (HEAD_DIM ** -0.5)
    k_all = qk[:, MIX_WIDTH:]

    gates = gt_ref[...] + gb_ref[...]
    row = lax.broadcasted_iota(jnp.int32, (8, L), 0)
    lgate = jnp.where(row < N_HEADS, gates, _log_sigmoid(gates))
    src = lax.broadcasted_iota(jnp.int32, (L, L), 0)
    dst = lax.broadcasted_iota(jnp.int32, (L, L), 1)
    upper = (src <= dst).astype(F32)
    csum = jnp.dot(lgate, upper, precision=HIGHEST, preferred_element_type=F32)
    rows8 = jnp.where(row < N_HEADS, gates, csum)
    cols = jnp.concatenate([rows8, jnp.zeros((L - 8, L), F32)], axis=0).T
    causal = dst <= src

    for h in range(N_HEADS):
        hs = slice(h * HEAD_DIM, (h + 1) * HEAD_DIM)
        li_row = rows8[h:h + 1, :]
        b_row = rows8[N_HEADS + h:N_HEADS + h + 1, :]
        li_col = cols[:, h:h + 1]
        b_col = cols[:, N_HEADS + h:N_HEADS + h + 1]
        m_prev = m_sc[h][0:1, 0:1]
        c_prev = c_sc[h]
        n_prev = n_sc[h][0:1, :]

        qh = q_all[:, hs]
        kh = k_all[:, hs]
        vh = zv_ref[:, hs]
        qh16 = qh.astype(BF16)
        kh16 = kh.astype(BF16)

        dmat = jnp.where(causal, b_col - b_row + li_row, NEG_INF)
        inter = b_col + m_prev
        m_row = jnp.maximum(inter, jnp.max(dmat, axis=-1, keepdims=True))
        w_intra = jnp.exp(dmat - m_row)
        w_inter = jnp.exp(inter - m_row)
        s = lax.dot_general(qh16, kh16, (((1,), (1,)), ((), ())), preferred_element_type=F32) * w_intra
        cq = lax.dot_general(qh16, c_prev.astype(BF16), (((1,), (1,)), ((), ())), preferred_element_type=F32)
        num = jnp.dot(s.astype(BF16), vh, preferred_element_type=F32) + w_inter * cq
        den = jnp.sum(s, axis=-1, keepdims=True) + w_inter * jnp.sum(qh * n_prev, axis=-1, keepdims=True)
        hm = num / jnp.maximum(jnp.abs(den), jnp.exp(-m_row))

        b_last = b_row[:, L - 1:L]
        g_row = b_last - b_row + li_row
        m_new = jnp.maximum(b_last + m_prev, jnp.max(g_row, axis=-1, keepdims=True))
        w_col = jnp.exp(b_last - b_col + li_col - m_new)
        decay = jnp.exp(b_last + m_prev - m_new)
        vw = (vh.astype(F32) * w_col).astype(BF16)
        c_sc[h] = decay * c_prev + lax.dot_general(vw, kh16, (((0,), (0,)), ((), ())), preferred_element_type=F32)
        n_new = decay * n_prev + jnp.sum(kh * w_col, axis=0, keepdims=True)
        n_sc[h] = jnp.broadcast_to(n_new, (8, HEAD_DIM))
        m_sc[h] = jnp.broadcast_to(m_new, (8, LANES))

        hm = _rms(hm, mlg_ref[:, hs])
        y_ref[:, hs] = (_sigmoid(zo_ref[:, hs].astype(F32)) * hm).astype(y_ref.dtype)

    u = _gelu_tanh(zu_ref[...].astype(F32))
    vs = _rms(_gelu_tanh(zs_ref[...].astype(F32)), sgg_ref[...]).astype(BF16)
    for g in range(N_HEADS):
        gs = slice(g * HEAD_DIM, (g + 1) * HEAD_DIM)
        wg = jnp.where(causal, sgw_ref[g], 0.0).astype(BF16)
        mixed = jnp.dot(wg, vs[:, gs], preferred_element_type=F32) + sgb_ref[:, g:g + 1]
        y_ref[:, MIX_WIDTH + g * HEAD_DIM:MIX_WIDTH + (g + 1) * HEAD_DIM] = (u[:, gs] * mixed).astype(y_ref.dtype)


def even_mixer_core(z, gates_t, gate_bias_col, conv_w, conv_b, ml_norm_g, sg_norm_g, sg_w, sg_b_t, batch):
    t = z.shape[0]
    nc = t // batch // CHUNK
    w = MIX_WIDTH

    def zspec(col):
        return pl.BlockSpec((CHUNK, w), lambda b, c, col=col: (b * nc + c, col))

    full = lambda shape: pl.BlockSpec(shape, lambda b, c: (0,) * len(shape))
    return pl.pallas_call(
        _even_mixer_kernel,
        out_shape=jax.ShapeDtypeStruct((t, 2 * w), BF16),
        grid=(batch, nc),
        in_specs=[zspec(0), zspec(1), zspec(2), zspec(3), zspec(4), zspec(5),
                  pl.BlockSpec((8, CHUNK), lambda b, c: (0, b * nc + c)),
                  full((8, 1)), full((CONV_TAPS, 2 * w)), full((1, 2 * w)),
                  full((1, w)), full((1, w)), full((N_HEADS, CHUNK, CHUNK)), full((CHUNK, N_HEADS))],
        out_specs=pl.BlockSpec((CHUNK, 2 * w), lambda b, c: (b * nc + c, 0)),
        scratch_shapes=[pltpu.VMEM((N_HEADS, HEAD_DIM, HEAD_DIM), F32),
                        pltpu.VMEM((N_HEADS, 8, HEAD_DIM), F32),
                        pltpu.VMEM((N_HEADS, 8, LANES), F32),
                        pltpu.VMEM((8, 2 * w), F32)],
        compiler_params=_params(("parallel", "arbitrary")),
        name="even_mixer",
    )(z, z, z, z, z, z, gates_t, gate_bias_col, conv_w, conv_b, ml_norm_g, sg_norm_g, sg_w, sg_b_t)


def _swiglu_accumulate(hn, wgu_ref, wd_ref, d_ff, chunk, after_chunk=None):
    acc = None
    n_chunks = d_ff // chunk
    for c in range(n_chunks):
        g = jnp.dot(hn, wgu_ref[:, c * chunk:(c + 1) * chunk], preferred_element_type=F32)
        u = jnp.dot(hn, wgu_ref[:, d_ff + c * chunk:d_ff + (c + 1) * chunk], preferred_element_type=F32)
        a = (g * _sigmoid(g) * u).astype(BF16)
        part = jnp.dot(a, wd_ref[c * chunk:(c + 1) * chunk, :], preferred_element_type=F32)
        acc = part if acc is None else acc + part
        if after_chunk is not None:
            after_chunk(c, n_chunks)
    return acc


def _dense_ffn_kernel(y_ref, wo_ref, res_ref, g_ref, wgu_ref, wd_ref, out_ref, *, d_ff, chunk):
    x = res_ref[...] + jnp.dot(y_ref[...], wo_ref[...], preferred_element_type=F32)
    hn = _rms(x, g_ref[...]).astype(BF16)
    out_ref[...] = x + _swiglu_accumulate(hn, wgu_ref, wd_ref, d_ff, chunk)


def proj_dense_ffn(y, w_out, res, g, w_gate_up, w_down):
    t, d = res.shape
    d_ff = w_down.shape[0]
    tm = min(TM_PROJ, t)
    chunk = FF_CHUNK_DENSE if d_ff % FF_CHUNK_DENSE == 0 else d_ff
    return pl.pallas_call(
        functools.partial(_dense_ffn_kernel, d_ff=d_ff, chunk=chunk),
        out_shape=jax.ShapeDtypeStruct((t, d), F32),
        grid=(t // tm,),
        in_specs=[pl.BlockSpec((tm, y.shape[1]), lambda i: (i, 0)),
                  pl.BlockSpec(w_out.shape, lambda i: (0, 0)),
                  pl.BlockSpec((tm, d), lambda i: (i, 0)),
                  pl.BlockSpec((1, d), lambda i: (0, 0)),
                  pl.BlockSpec(w_gate_up.shape, lambda i: (0, 0)),
                  pl.BlockSpec(w_down.shape, lambda i: (0, 0))],
        out_specs=pl.BlockSpec((tm, d), lambda i: (i, 0)),
        compiler_params=_params(("parallel",), VMEM_LIMIT_MOE),
        name="dense_ffn",
    )(y, w_out, res, g, w_gate_up, w_down)


def _odd_prep_kernel(dq_ref, dk_ref, gz_ref, fb_ref, ca_ref, sm_ref, sp_ref,
                     q1_ref, q2_ref, kr_ref, f_ref, carry_sc):
    tm = dq_ref.shape[0]

    @pl.when(pl.program_id(1) == 0)
    def _():
        carry_sc[...] = jnp.zeros_like(carry_sc)

    cos_a = ca_ref[...]
    sin_m = sm_ref[...]
    sin_p = sp_ref[...]
    lane = lax.broadcasted_iota(jnp.int32, (tm, LANES), 1)
    first_map = lane < DIFF_QK
    for h in range(N_HEADS):
        hs = slice(h * LANES, (h + 1) * LANES)
        for src_ref, is_q in ((dq_ref, True), (dk_ref, False)):
            x = src_ref[:, hs].astype(F32)
            rot = (x * cos_a + pltpu.roll(x, LANES - ROPE_DIMS // 2, axis=1) * sin_m
                   + pltpu.roll(x, ROPE_DIMS // 2, axis=1) * sin_p)
            if is_q:
                q1_ref[:, hs] = jnp.where(first_map, rot, 0.0).astype(q1_ref.dtype)
                q2_ref[:, hs] = jnp.where(first_map, 0.0, rot).astype(q2_ref.dtype)
            else:
                kr_ref[:, hs] = rot.astype(kr_ref.dtype)

    lf = _log_sigmoid(gz_ref[...] + fb_ref[...])
    r = lax.broadcasted_iota(jnp.int32, (tm, tm), 0)
    c = lax.broadcasted_iota(jnp.int32, (tm, tm), 1)
    lower = (c <= r).astype(F32)
    fcum = jnp.dot(lower, lf, precision=HIGHEST, preferred_element_type=F32) + carry_sc[0:1, :]
    f_ref[...] = fcum * LOG2E
    carry_sc[...] = jnp.broadcast_to(fcum[tm - 1:tm, :], carry_sc.shape)


def odd_prep(z, gz, f_bias_row, cos_a, sin_m, sin_p, batch):
    t = z.shape[0]
    seq = t // batch
    tm = min(TM_PROJ, seq)
    ns = seq // tm
    w = MIX_WIDTH
    tok = lambda col: pl.BlockSpec((tm, w), lambda b, s, col=col: (b * ns + s, col))
    tab = pl.BlockSpec((tm, LANES), lambda b, s: (s, 0))
    out_tok = pl.BlockSpec((tm, w), lambda b, s: (b * ns + s, 0))
    return pl.pallas_call(
        _odd_prep_kernel,
        out_shape=(jax.ShapeDtypeStruct((t, w), BF16), jax.ShapeDtypeStruct((t, w), BF16),
                   jax.ShapeDtypeStruct((t, w), BF16), jax.ShapeDtypeStruct((t, LANES), F32)),
        grid=(batch, ns),
        in_specs=[tok(3), tok(4),
                  pl.BlockSpec((tm, LANES), lambda b, s: (b * ns + s, 0)),
                  pl.BlockSpec((1, LANES), lambda b, s: (0, 0)),
                  tab, tab, tab],
        out_specs=(out_tok, out_tok, out_tok, pl.BlockSpec((tm, LANES), lambda b, s: (b * ns + s, 0))),
        scratch_shapes=[pltpu.VMEM((8, LANES), F32)],
        compiler_params=_params(("parallel", "arbitrary")),
        name="odd_prep",
    )(z, z, gz, f_bias_row, cos_a, sin_m, sin_p)


def _softmax_chunk(q, k, v, fk, fq, m_ref, l_ref, acc_ref, mask):
    s = lax.dot_general(q, k, (((1,), (1,)), ((), ())), preferred_element_type=F32)
    if fk is not None:
        s = s - fk
    if mask is not None:
        s = jnp.where(mask, s, NEG_INF)
    row_max = jnp.max(s, axis=-1, keepdims=True)
    if fq is not None:
        row_max = row_max + fq
    m_prev = m_ref[...]
    m_new = jnp.maximum(m_prev, row_max)
    p = jnp.exp2(s - (m_new if fq is None else m_new - fq))
    alpha = jnp.exp2(m_prev - m_new)
    l_ref[...] = alpha * l_ref[...] + jnp.sum(p, axis=-1, keepdims=True)
    acc_ref[...] = alpha * acc_ref[...] + jnp.dot(p.astype(BF16), v, preferred_element_type=F32)
    m_ref[...] = m_new


def _causal_sweep(i, tq, ch, streams):
    for st in streams:
        st["m"][...] = jnp.full_like(st["m"], -1e30)
        st["l"][...] = jnp.zeros_like(st["l"])
        st["acc"][...] = jnp.zeros_like(st["acc"])

    def chunk(j, mask):
        start = pl.multiple_of(j * ch, ch)
        for st in streams:
            fk = st["fk"](j) if st["fk"] is not None else None
            _softmax_chunk(st["q"], st["k"](start), st["v"](start), fk, st["fq"], st["m"], st["l"], st["acc"], mask)

    def body(j, carry):
        chunk(j, None)
        return carry

    n_full = (i * tq) // ch
    lax.fori_loop(0, n_full, body, 0)
    r = lax.broadcasted_iota(jnp.int32, (tq, ch), 0)
    c = lax.broadcasted_iota(jnp.int32, (tq, ch), 1)
    chunk(n_full, (c - r) <= (i * tq - n_full * ch))


def _fox_attention_kernel(q_ref, k_ref, v_ref, fcol_ref, frow_ref, out_ref, *scratch, tq, ch):
    hp = pl.program_id(1)
    i = pl.program_id(2)
    lane = lax.broadcasted_iota(jnp.int32, (tq, LANES), 1)
    fcol = fcol_ref[...]
    streams = []
    for n in range(2):
        hs = slice(n * LANES, (n + 1) * LANES)
        streams.append(dict(
            q=q_ref[:, hs],
            k=lambda start, hs=hs: k_ref[pl.ds(start, ch), hs],
            v=lambda start, hs=hs: v_ref[pl.ds(start, ch), hs],
            fk=lambda j, n=n: frow_ref[n, j],
            fq=jnp.sum(jnp.where(lane == 2 * hp + n, fcol, 0.0), axis=-1, keepdims=True),
            m=scratch[3 * n], l=scratch[3 * n + 1], acc=scratch[3 * n + 2]))
    _causal_sweep(i, tq, ch, streams)
    for n, st in enumerate(streams):
        out_ref[:, n * LANES:(n + 1) * LANES] = (st["acc"][...] / st["l"][...]).astype(out_ref.dtype)


def _diff_attention_kernel(q1_ref, q2_ref, k_ref, v_ref, lam_ref, ng_ref, out_ref, *scratch, tq, ch, lambda_init):
    i = pl.program_id(2)
    streams = []
    for n, q_ref in enumerate((q1_ref, q2_ref)):
        streams.append(dict(
            q=q_ref[...],
            k=lambda start: k_ref[pl.ds(start, ch), :],
            v=lambda start: v_ref[pl.ds(start, ch), :],
            fk=None, fq=None,
            m=scratch[3 * n], l=scratch[3 * n + 1], acc=scratch[3 * n + 2]))
    _causal_sweep(i, tq, ch, streams)
    o1, o2 = (st["acc"][...] / st["l"][...] for st in streams)
    lam_p = lam_ref[...]
    lam = (jnp.exp(jnp.sum(lam_p[0:1, :] * lam_p[1:2, :], axis=-1, keepdims=True))
           - jnp.exp(jnp.sum(lam_p[2:3, :] * lam_p[3:4, :], axis=-1, keepdims=True)) + lambda_init)
    out_ref[...] = (_rms(o1 - lam * o2, ng_ref[...]) * (1.0 - lambda_init)).astype(out_ref.dtype)


def _attention_tiles(seq):
    tq = min(ATT_TQ, seq)
    ch = min(ATT_CHUNK, seq)
    assert seq % ch == 0 and ch % tq == 0
    return tq, ch


def _stream_scratch(tq):
    return [pltpu.VMEM((tq, 1), F32), pltpu.VMEM((tq, 1), F32), pltpu.VMEM((tq, LANES), F32)] * 2


def fox_attention(z, fcol, frow, batch):
    t = z.shape[0]
    seq = t // batch
    tq, ch = _attention_tiles(seq)
    nq = seq // tq
    pairs = N_HEADS // 2
    w2 = 2 * LANES
    return pl.pallas_call(
        functools.partial(_fox_attention_kernel, tq=tq, ch=ch),
        out_shape=jax.ShapeDtypeStruct((t, MIX_WIDTH), BF16),
        grid=(batch, pairs, nq),
        in_specs=[pl.BlockSpec((tq, w2), lambda b, hp, i: (b * nq + i, hp)),
                  pl.BlockSpec((seq, w2), lambda b, hp, i: (b, pairs + hp)),
                  pl.BlockSpec((seq, w2), lambda b, hp, i: (b, 2 * pairs + hp)),
                  pl.BlockSpec((tq, LANES), lambda b, hp, i: (b * nq + i, 0)),
                  pl.BlockSpec((2, seq // ch, 1, ch), lambda b, hp, i: (b * pairs + hp, 0, 0, 0))],
        out_specs=pl.BlockSpec((tq, w2), lambda b, hp, i: (b * nq + i, hp)),
        scratch_shapes=_stream_scratch(tq),
        compiler_params=_params(("parallel", "parallel", "arbitrary"), VMEM_LIMIT_MOE),
        name="fox_attention",
    )(z, z, z, fcol, frow)


def diff_attention(q1, q2, kr, z, v_col0, lam_p, ng, lambda_init, batch):
    t = z.shape[0]
    seq = t // batch
    tq, ch = _attention_tiles(seq)
    nq = seq // tq
    qspec = pl.BlockSpec((tq, LANES), lambda b, h, i: (b * nq + i, h))
    return pl.pallas_call(
        functools.partial(_diff_attention_kernel, tq=tq, ch=ch, lambda_init=lambda_init),
        out_shape=jax.ShapeDtypeStruct((t, MIX_WIDTH), BF16),
        grid=(batch, N_HEADS, nq),
        in_specs=[qspec, qspec,
                  pl.BlockSpec((seq, LANES), lambda b, h, i: (b, h)),
                  pl.BlockSpec((seq, LANES), lambda b, h, i: (b, v_col0 + h)),
                  pl.BlockSpec(lam_p.shape, lambda b, h, i: (0, 0)),
                  pl.BlockSpec(ng.shape, lambda b, h, i: (0, 0))],
        out_specs=qspec,
        scratch_shapes=_stream_scratch(tq),
        compiler_params=_params(("parallel", "parallel", "arbitrary"), VMEM_LIMIT_MOE),
        name="diff_attention",
    )(q1, q2, kr, z, lam_p, ng)


def _router_kernel(ya_ref, yb_ref, wa_ref, wb_ref, res_ref, g_ref, wr_ref, x_ref, hp_ref, meta_ref, cnt_ref,
                   carry_sc):
    tm = res_ref.shape[0]

    @pl.when(pl.program_id(0) == 0)
    def _():
        carry_sc[...] = jnp.zeros_like(carry_sc)

    x = (res_ref[...] + jnp.dot(ya_ref[...], wa_ref[...], preferred_element_type=F32)
         + jnp.dot(yb_ref[...], wb_ref[...], preferred_element_type=F32))
    x_ref[...] = x
    hn = _rms(x, g_ref[...])
    hp_ref[...] = hn
    lane = lax.broadcasted_iota(jnp.int32, (tm, LANES), 1)
    h_hi = hn.astype(BF16)
    h_lo = (hn - h_hi.astype(F32)).astype(BF16)
    wr = wr_ref[...]
    w_hi = wr.astype(BF16)
    w_lo = (wr - w_hi.astype(F32)).astype(BF16)
    logits = (jnp.dot(h_hi, w_hi, preferred_element_type=F32)
              + (jnp.dot(h_lo, w_hi, preferred_element_type=F32) + jnp.dot(h_hi, w_lo, preferred_element_type=F32)))
    logits = jnp.where(lane < N_EXPERTS, logits, NEG_INF)
    m1 = jnp.max(logits, axis=-1, keepdims=True)
    i1 = jnp.min(jnp.where(logits == m1, lane, LANES), axis=-1, keepdims=True)
    rest = jnp.where(lane == i1, NEG_INF, logits)
    m2 = jnp.max(rest, axis=-1, keepdims=True)
    i2 = jnp.min(jnp.where(rest == m2, lane, LANES), axis=-1, keepdims=True)
    e2 = jnp.exp(m2 - m1)
    w1 = 1.0 / (1.0 + e2)
    w2 = e2 * w1

    sel1 = lane == i1
    sel2 = lane == i2
    onehot = jnp.where(sel1, 1.0, 0.0) + jnp.where(sel2, 1.0, 0.0)
    r = lax.broadcasted_iota(jnp.int32, (tm, tm), 0)
    c = lax.broadcasted_iota(jnp.int32, (tm, tm), 1)
    lower = (c <= r).astype(BF16)
    incl = jnp.dot(lower, onehot.astype(BF16), preferred_element_type=F32)
    before = incl - onehot + carry_sc[0:1, :]
    r1 = jnp.sum(jnp.where(sel1, before, 0.0), axis=-1, keepdims=True)
    r2 = jnp.sum(jnp.where(sel2, before, 0.0), axis=-1, keepdims=True)
    total = incl[tm - 1:tm, :] + carry_sc[0:1, :]
    carry_sc[...] = jnp.broadcast_to(total, carry_sc.shape)
    cnt_ref[...] = jnp.broadcast_to(total, cnt_ref.shape)

    meta = jnp.where(lane == 0, i1.astype(F32), 0.0)
    meta = jnp.where(lane == 1, i2.astype(F32), meta)
    meta = jnp.where(lane == 2, r1, meta)
    meta = jnp.where(lane == 3, r2, meta)
    meta = jnp.where(lane == 4, w1, meta)
    meta = jnp.where(lane == 5, w2, meta)
    meta_ref[...] = meta


def moe_router(ya, yb, wa, wb, res, g, w_router_pad):
    t, d = res.shape
    tm = min(TM_PROJ, t)
    tok = lambda width: pl.BlockSpec((tm, width), lambda i: (i, 0))
    full = lambda a: pl.BlockSpec(a.shape, lambda i: (0, 0))
    return pl.pallas_call(
        _router_kernel,
        out_shape=(jax.ShapeDtypeStruct((t, d), F32), jax.ShapeDtypeStruct((t, d), F32),
                   jax.ShapeDtypeStruct((t, LANES), F32), jax.ShapeDtypeStruct((8, LANES), F32)),
        grid=(t // tm,),
        in_specs=[tok(ya.shape[1]), tok(yb.shape[1]), full(wa), full(wb), tok(d), full(g), full(w_router_pad)],
        out_specs=(tok(d), tok(d), tok(LANES), pl.BlockSpec((8, LANES), lambda i: (0, 0))),
        scratch_shapes=[pltpu.VMEM((8, LANES), F32)],
        compiler_params=_params(("arbitrary",)),
        name="moe_router",
    )(ya, yb, wa, wb, res, g, w_router_pad)


def _expert_ffn_kernel(te_ref, nt_ref, tok_ref, dst_ref, hn_ref, wgu_ref, wd_ref, planes_ref,
                       xbuf, ybuf, hbuf, gsem, ssem, zsem, *, d_ff, chunk, tm, n_pad_tiles):
    del te_ref
    i = pl.program_id(0)
    nt = nt_ref[0]
    n_real = planes_ref.shape[0] - (n_pad_tiles + 1) * tm

    def gather_row(tile, r):
        return pltpu.make_async_copy(hn_ref.at[pl.ds(tok_ref[tile * tm + r], 1)], xbuf.at[pl.ds(r, 1)], gsem)

    def scatter_row(tile, r):
        return pltpu.make_async_copy(ybuf.at[pl.ds(r, 1)], planes_ref.at[pl.ds(dst_ref[(tile + 1) * tm + r], 1)], ssem)

    def start_rows(make_copy, tile):
        def body(r, carry):
            make_copy(tile, r).start()
            return carry

        lax.fori_loop(0, tm, body, 0, unroll=8)

    def wait_gather():
        pltpu.make_async_copy(hn_ref.at[pl.ds(0, tm)], xbuf, gsem).wait()

    def wait_scatter():
        pltpu.make_async_copy(ybuf, planes_ref.at[pl.ds(0, tm)], ssem).wait()

    def zero_fill(p):
        return pltpu.make_async_copy(ybuf, planes_ref.at[pl.ds(n_real + p * tm, tm)], zsem)

    @pl.when(i == 0)
    def _():
        start_rows(gather_row, 0)
        ybuf[...] = jnp.zeros_like(ybuf)
        for p in range(n_pad_tiles):
            zero_fill(p).start()
        for p in range(n_pad_tiles):
            zero_fill(p).wait()

    @pl.when(i < nt)
    def _():
        wait_gather()
        hbuf[...] = xbuf[...].astype(BF16)

        def issue_rows(c, n_chunks):
            n_issue = n_chunks - 1
            if c < n_issue:
                for r in range(c * tm // n_issue, (c + 1) * tm // n_issue):
                    gather_row(i + 1, r).start()
                    scatter_row(i - 1, r).start()

        acc = _swiglu_accumulate(hbuf[...], wgu_ref.at[0], wd_ref.at[0], d_ff, chunk, after_chunk=issue_rows)
        wait_scatter()
        ybuf[...] = acc

    @pl.when(i == nt)
    def _():
        wait_gather()
        start_rows(scatter_row, i - 1)
        wait_scatter()


def moe_expert_ffn(tile_expert, n_tiles, row_token, row_dst, hn, w_gate_up, w_down):
    t, d = hn.shape
    n_exp, d_ff = w_down.shape[:2]
    tm = min(TM_MOE, 2 * t)
    n_grid = row_token.shape[0] // tm
    chunk = FF_CHUNK_MOE if d_ff % FF_CHUNK_MOE == 0 else d_ff
    w_index = lambda i, te, nt, tok, dst: (te[i], 0, 0)
    return pl.pallas_call(
        functools.partial(_expert_ffn_kernel, d_ff=d_ff, chunk=chunk, tm=tm, n_pad_tiles=n_exp),
        out_shape=jax.ShapeDtypeStruct((2 * t + (n_exp + 1) * tm, d), F32),
        grid_spec=pltpu.PrefetchScalarGridSpec(
            num_scalar_prefetch=4,
            grid=(n_grid,),
            in_specs=[pl.BlockSpec(memory_space=pl.ANY),
                      pl.BlockSpec((1, d, 2 * d_ff), w_index),
                      pl.BlockSpec((1, d_ff, d), w_index)],
            out_specs=pl.BlockSpec(memory_space=pl.ANY),
            scratch_shapes=[pltpu.VMEM((tm, d), F32), pltpu.VMEM((tm, d), F32), pltpu.VMEM((tm, d), BF16),
                            pltpu.SemaphoreType.DMA(()), pltpu.SemaphoreType.DMA(()),
                            pltpu.SemaphoreType.DMA(())]),
        compiler_params=_params(("arbitrary",), VMEM_LIMIT_MOE),
        name="moe_expert_ffn",
    )(tile_expert, n_tiles, row_token, row_dst, hn, w_gate_up, w_down)


def _combine_kernel(x_ref, meta_ref, g_ref, p0_ref, p1_ref, out_ref):
    meta = meta_ref[...]
    y = x_ref[...] + meta[:, 4:5] * p0_ref[...] + meta[:, 5:6] * p1_ref[...]
    out_ref[...] = _rms(y, g_ref[...])


def moe_combine(x, meta, g_final, planes):
    t, d = x.shape
    tm = min(TM_PROJ, t)
    nb = t // tm
    return pl.pallas_call(
        _combine_kernel,
        out_shape=jax.ShapeDtypeStruct((t, d), F32),
        grid=(nb,),
        in_specs=[pl.BlockSpec((tm, d), lambda i: (i, 0)),
                  pl.BlockSpec((tm, LANES), lambda i: (i, 0)),
                  pl.BlockSpec((1, d), lambda i: (0, 0)),
                  pl.BlockSpec((tm, d), lambda i: (i, 0)),
                  pl.BlockSpec((tm, d), lambda i: (nb + i, 0))],
        out_specs=pl.BlockSpec((tm, d), lambda i: (i, 0)),
        compiler_params=_params(("parallel",)),
        name="moe_combine",
    )(x, meta, g_final, planes, planes)


def moe_layer(ya, yb, wa, wb, res, g_ffn, w_router, w_gate_up, w_down, g_final):
    t, d = res.shape
    n_exp = w_router.shape[1]
    tm = min(TM_MOE, 2 * t)
    n_rows = 2 * t + n_exp * tm
    wr_pad = jnp.zeros((d, LANES), F32).at[:, :n_exp].set(w_router)
    x, hn, meta, counts = moe_router(ya, yb, wa, wb, res, g_ffn, wr_pad)

    cnt = counts[0, :n_exp].astype(jnp.int32)
    padded = ((cnt + tm - 1) // tm) * tm
    ends = jnp.cumsum(padded)
    starts = ends - padded
    e_idx = meta[:, 0:2].astype(jnp.int32)
    rank = meta[:, 2:4].astype(jnp.int32)
    start_of = jnp.sum(jnp.where(e_idx[..., None] == jnp.arange(n_exp), starts, 0), axis=-1)
    pos = (start_of + rank).T.reshape(-1)
    tile_start = jnp.arange(n_rows // tm, dtype=jnp.int32) * tm
    tile_expert = jnp.minimum(jnp.sum(tile_start[:, None] >= ends[None, :], axis=-1), n_exp - 1).astype(jnp.int32)
    n_tiles = (ends[-1:] // tm).astype(jnp.int32)
    pad_dst = (2 * t + tile_expert[:, None] * tm + jnp.arange(tm, dtype=jnp.int32)[None, :]).reshape(-1)
    served = jnp.full((n_rows,), -1, jnp.int32).at[pos].set(jnp.arange(2 * t, dtype=jnp.int32), unique_indices=True)
    zero_tile_dst = 2 * t + n_exp * tm + jnp.arange(tm, dtype=jnp.int32)
    row_dst = jnp.concatenate([zero_tile_dst, jnp.where(served >= 0, served, pad_dst)])
    row_token = jnp.where(served >= 0, served % t, 0)

    planes = moe_expert_ffn(tile_expert, n_tiles, row_token, row_dst, hn, w_gate_up, w_down)
    return moe_combine(x, meta, g_final, planes)


def _rope_tables(seq):
    half = ROPE_DIMS // 2
    inv = ROPE_THETA ** (-jnp.arange(half, dtype=F32) / half)
    ang = jnp.arange(seq, dtype=F32)[:, None] * inv[None, :]
    cos, sin = jnp.cos(ang), jnp.sin(ang)
    ones = jnp.ones((seq, DIFF_QK - ROPE_DIMS), F32)
    zeros = jnp.zeros((seq, DIFF_QK - ROPE_DIMS), F32)
    zh = jnp.zeros((seq, half), F32)
    cos_a = jnp.concatenate([cos, cos, ones], axis=1)
    sin_m = jnp.concatenate([-sin, zh, zeros], axis=1)
    sin_p = jnp.concatenate([zh, sin, zeros], axis=1)
    tile2 = lambda a: jnp.concatenate([a, a], axis=1)
    return tile2(cos_a), tile2(sin_m), tile2(sin_p)


def even_layer(x, batch, norm_mix, w_in, conv_w, conv_b, gate_b, ml_norm_g, sg_norm_g, sg_w, sg_b, w_out,
               norm_ffn, w_gate_up, w_down):
    w = MIX_WIDTH
    n_gate = 2 * N_HEADS
    w_main = jnp.concatenate([w_in[:, :4 * w].astype(BF16), w_in[:, 4 * w + n_gate:].astype(BF16)], axis=1)
    w_gate = jnp.zeros((w_in.shape[0], LANES), F32).at[:, :n_gate].set(w_in[:, 4 * w:4 * w + n_gate]).astype(BF16)
    z, gz = norm_proj(x, norm_mix[None, :], w_main, w_gate)
    gates_t = gz[:, :n_gate].T
    y = even_mixer_core(z, gates_t, gate_b.reshape(n_gate, 1), conv_w, conv_b[None, :], ml_norm_g[None, :],
                        sg_norm_g[None, :], sg_w, sg_b.T, batch)
    return proj_dense_ffn(y, w_out.astype(BF16), x, norm_ffn[None, :], w_gate_up.astype(BF16), w_down.astype(BF16))


def odd_mixer_layer(x, batch, norm_mix, w_in, fox_f_b, diff_lambda, diff_norm_g, w_out, lambda_init):
    w = MIX_WIDTH
    t = x.shape[0]
    seq = t // batch
    w_main = jnp.concatenate([(w_in[:, :w] * (LOG2E * HEAD_DIM ** -0.5)).astype(BF16),
                              w_in[:, w:3 * w].astype(BF16),
                              (w_in[:, 3 * w + N_HEADS:4 * w + N_HEADS] * (LOG2E * DIFF_QK ** -0.5)).astype(BF16),
                              w_in[:, 4 * w + N_HEADS:].astype(BF16)], axis=1)
    w_gate = jnp.zeros((w_in.shape[0], LANES), F32).at[:, :N_HEADS].set(w_in[:, 3 * w:3 * w + N_HEADS]).astype(BF16)
    z, gz = norm_proj(x, norm_mix[None, :], w_main, w_gate)
    f_bias_row = jnp.zeros((1, LANES), F32).at[0, :N_HEADS].set(fox_f_b)
    cos_a, sin_m, sin_p = _rope_tables(seq)
    q1, q2, kr, fcol = odd_prep(z, gz, f_bias_row, cos_a, sin_m, sin_p, batch)
    _, ch = _attention_tiles(seq)
    frow = fcol[:, :N_HEADS].reshape(batch, seq, N_HEADS).transpose(0, 2, 1).reshape(
        batch * N_HEADS, seq // ch, 1, ch)
    y_fox = fox_attention(z, fcol, frow, batch)
    y_diff = diff_attention(q1, q2, kr, z, 5 * N_HEADS, diff_lambda, diff_norm_g[None, :], lambda_init, batch)
    w_out16 = w_out.astype(BF16)
    return y_fox, y_diff, w_out16[:w], w_out16[w:]


def kernel(x, even_norm_mix, even_w_in, even_ml_conv_w, even_ml_conv_b, even_ml_gate_b, even_ml_norm_g,
           even_sg_norm_g, even_sg_w, even_sg_b, even_w_out, even_norm_ffn, ffn_w_gate_up, ffn_w_down,
           odd_norm_mix, odd_w_in, odd_fox_f_b, odd_diff_lambda, odd_diff_norm_g, odd_w_out, odd_norm_ffn,
           moe_w_router, moe_w_gate_up, moe_w_down, final_norm):
    batch, seq, d = x.shape
    h = x.reshape(batch * seq, d)
    h = even_layer(h, batch, even_norm_mix[0], even_w_in[0], even_ml_conv_w[0], even_ml_conv_b[0],
                   even_ml_gate_b[0], even_ml_norm_g[0], even_sg_norm_g[0], even_sg_w[0], even_sg_b[0],
                   even_w_out[0], even_norm_ffn[0], ffn_w_gate_up[0], ffn_w_down[0])
    lambda_init = 0.8 - 0.6 * math.exp(-0.3 * 1)
    mixed = odd_mixer_layer(h, batch, odd_norm_mix[0], odd_w_in[0], odd_fox_f_b[0], odd_diff_lambda[0],
                            odd_diff_norm_g[0], odd_w_out[0], lambda_init)
    out = moe_layer(*mixed, h, odd_norm_ffn[0][None, :], moe_w_router[0], moe_w_gate_up[0].astype(BF16),
                    moe_w_down[0].astype(BF16), final_norm[None, :])
    return out.reshape(batch, seq, d)
```
